```python
import jax, jax.numpy as jnp
from jax import lax
import numpy as np

D_MODEL = 2048
BATCH = 16
SEQ = 2048
DEPTH = 1
DEC_BATCH = 8
DEC_SEQ = 4096
PAST_LEN = 128

HEAD_DIM = 64
N_HEADS_A = 16
N_HEADS_B = 16
N_KV_B = 4
WIDTH_A = N_HEADS_A * HEAD_DIM
WIDTH_B = N_HEADS_B * HEAD_DIM
KV_WIDTH_B = N_KV_B * HEAD_DIM
MIX_WIDTH = WIDTH_A + WIDTH_B
PROJ_WIDTH = 3 * WIDTH_A + WIDTH_B + 2 * KV_WIDTH_B
GRID_W = 64
NA_KH = 8
NA_KW = 16
WINDOW = 128
WBLOCK = 128
ROT_DIM = HEAD_DIM // 4
ROPE_THETA = 500000.0
N_META = 16
N_EXPERTS = 32
TOP_K = 4
D_FF = 2048
SWIGLU_LIMIT = 7.0
SWIGLU_ALPHA = 1.702
MOE_BLOCK = 256
NORM_EPS = 1e-5
NEG_INF = -1e30

kernel_name = 'hymba_natten_swa_moe_encoder'


def rmsnorm(x, g):
    xf = x.astype(jnp.float32)
    y = xf * lax.rsqrt(jnp.mean(xf * xf, axis=-1, keepdims=True) + NORM_EPS)
    return (y * g.astype(jnp.float32)).astype(x.dtype)


def partial_rope(x, positions):
    inv = jnp.asarray(ROPE_THETA ** (-np.arange(0, ROT_DIM, 2, dtype=np.float32) / ROT_DIM), dtype=jnp.float32)
    ang = positions.astype(jnp.float32)[:, None] * inv[None, :]
    cos = jnp.cos(ang)[None, :, None, :]
    sin = jnp.sin(ang)[None, :, None, :]
    xf = x.astype(jnp.float32)
    x1 = xf[..., :ROT_DIM // 2]
    x2 = xf[..., ROT_DIM // 2:ROT_DIM]
    out = jnp.concatenate([x1 * cos - x2 * sin, x2 * cos + x1 * sin, xf[..., ROT_DIM:]], axis=-1)
    return out.astype(x.dtype)


def neighbourhood_attention(q, k, v, q_gain, k_gain, rpb, rpb_meta):
    B, L, H, Dh = q.shape
    T = L - N_META
    rows = T // GRID_W
    kh = min(NA_KH, rows)
    q = rmsnorm(q, q_gain) * (Dh ** -0.5)
    k = rmsnorm(k, k_gain)
    qm, km, vm = q[:, :N_META], k[:, :N_META], v[:, :N_META]
    qg = q[:, N_META:].reshape(B, rows, GRID_W, H, Dh)
    kg = k[:, N_META:].reshape(B, rows, GRID_W, H, Dh)
    vg = v[:, N_META:].reshape(B, rows, GRID_W, H, Dh)
    r = np.arange(rows)
    row_start = np.clip(r - kh // 2, 0, rows - kh)
    band_rows = row_start[:, None] + np.arange(kh)[None, :]
    kband = kg[:, band_rows]
    vband = vg[:, band_rows]
    cq = np.arange(GRID_W)
    col_start = np.clip(cq - NA_KW // 2, 0, GRID_W - NA_KW)
    col_valid = (cq[None, :] >= col_start[:, None]) & (cq[None, :] < col_start[:, None] + NA_KW)
    dr = band_rows - r[:, None]
    dc = np.clip(cq[None, :] - cq[:, None], -(NA_KW - 1), NA_KW - 1)
    ridx = (dr + NA_KH - 1)[:, None, :, None]
    cidx = (dc + NA_KW - 1)[None, :, None, :]
    bias = rpb[:, ridx, cidx].astype(jnp.float32)
    bias = jnp.where(col_valid[None, None, :, None, :], bias, NEG_INF)
    s_grid = jnp.einsum('brchd,brjwhd->bhrcjw', qg, kband, preferred_element_type=jnp.float32) + bias[None]
    s_grid = s_grid.reshape(B, H, rows, GRID_W, kh * GRID_W)
    s_meta = jnp.einsum('brchd,bmhd->bhrcm', qg, km, preferred_element_type=jnp.float32)
    s_meta = s_meta + rpb_meta.astype(jnp.float32)[None, :, None, None, :]
    p = jax.nn.softmax(jnp.concatenate([s_grid, s_meta], axis=-1), axis=-1).astype(v.dtype)
    nk = kh * GRID_W
    p_grid = p[..., :nk].reshape(B, H, rows, GRID_W, kh, GRID_W)
    og = jnp.einsum('bhrcjw,brjwhd->brchd', p_grid, vband) + jnp.einsum('bhrcm,bmhd->brchd', p[..., nk:], vm)
    og = og.reshape(B, T, H * Dh)
    s_mm = jnp.einsum('bqhd,bmhd->bhqm', qm, km, preferred_element_type=jnp.float32)
    s_mm = s_mm + rpb_meta.astype(jnp.float32)[None, :, None, :]
    p_mm = jax.nn.softmax(s_mm, axis=-1).astype(v.dtype)
    om = jnp.einsum('bhqm,bmhd->bqhd', p_mm, vm).reshape(B, N_META, H * Dh)
    return jnp.concatenate([om, og], axis=1)


def window_attention(q, k, v, q_gain, k_gain, sinks):
    B, L, H, Dh = q.shape
    KV = k.shape[2]
    G = H // KV
    T = L - N_META
    nb = T // WBLOCK
    pos = jnp.arange(L)
    q = partial_rope(rmsnorm(q, q_gain), pos) * (Dh ** -0.5)
    k = partial_rope(rmsnorm(k, k_gain), pos)
    q = q.reshape(B, L, KV, G, Dh)
    qm, km, vm = q[:, :N_META], k[:, :N_META], v[:, :N_META]
    qr = q[:, N_META:].reshape(B, nb, WBLOCK, KV, G, Dh)
    kr = k[:, N_META:].reshape(B, nb, WBLOCK, KV, Dh)
    vr = v[:, N_META:].reshape(B, nb, WBLOCK, KV, Dh)
    pad = ((0, 0), (1, 1), (0, 0), (0, 0), (0, 0))
    kp = jnp.pad(kr, pad)
    vp = jnp.pad(vr, pad)
    kband = jnp.concatenate([kp[:, :-2], kp[:, 1:-1], kp[:, 2:]], axis=2)
    vband = jnp.concatenate([vp[:, :-2], vp[:, 1:-1], vp[:, 2:]], axis=2)
    bi = np.arange(nb)[:, None, None]
    qa = np.arange(WBLOCK)[None, :, None]
    kj = np.arange(3 * WBLOCK)[None, None, :]
    tk = (bi - 1) * WBLOCK + kj
    tq = bi * WBLOCK + qa
    valid = (tk >= 0) & (tk < T) & (np.abs(tk - tq) <= WINDOW)
    sink = sinks.astype(jnp.float32).reshape(KV, G)
    s_band = jnp.einsum('bnqkgd,bnjkd->bkgnqj', qr, kband, preferred_element_type=jnp.float32)
    s_band = jnp.where(valid[None, None, None], s_band, NEG_INF)
    s_meta = jnp.einsum('bnqkgd,bmkd->bkgnqm', qr, km, preferred_element_type=jnp.float32)
    s_sink = jnp.broadcast_to(sink[None, :, :, None, None, None], (B, KV, G, nb, WBLOCK, 1))
    p = jax.nn.softmax(jnp.concatenate([s_band, s_meta, s_sink], axis=-1), axis=-1).astype(v.dtype)
    nk = 3 * WBLOCK
    o_r = jnp.einsum('bkgnqj,bnjkd->bnqkgd', p[..., :nk], vband) + jnp.einsum('bkgnqm,bmkd->bnqkgd', p[..., nk:nk + N_META], vm)
    o_r = o_r.reshape(B, T, H * Dh)
    valid_mr = (N_META + np.arange(WBLOCK)[None, :]) - np.arange(N_META)[:, None] <= WINDOW
    s_mm = jnp.einsum('bqkgd,bmkd->bkgqm', qm, km, preferred_element_type=jnp.float32)
    s_mr = jnp.einsum('bqkgd,bjkd->bkgqj', qm, kr[:, 0], preferred_element_type=jnp.float32)
    s_mr = jnp.where(valid_mr[None, None, None], s_mr, NEG_INF)
    s_ms = jnp.broadcast_to(sink[None, :, :, None, None], (B, KV, G, N_META, 1))
    p_m = jax.nn.softmax(jnp.concatenate([s_mm, s_mr, s_ms], axis=-1), axis=-1).astype(v.dtype)
    o_m = jnp.einsum('bkgqm,bmkd->bqkgd', p_m[..., :N_META], vm) + jnp.einsum('bkgqj,bjkd->bqkgd', p_m[..., N_META:N_META + WBLOCK], vr[:, 0])
    return jnp.concatenate([o_m.reshape(B, N_META, H * Dh), o_r], axis=1)


def token_mixing(x, norm_g, w_in, q_norm_a, k_norm_a, rpb, rpb_meta, q_norm_b, k_norm_b, sinks, out_norm_a, out_norm_b, w_out):
    B, L, _ = x.shape
    h = rmsnorm(x, norm_g)
    proj = h @ w_in
    cuts = [WIDTH_A, 2 * WIDTH_A, 3 * WIDTH_A, 3 * WIDTH_A + WIDTH_B, 3 * WIDTH_A + WIDTH_B + KV_WIDTH_B]
    qa, ka, va, qb, kb, vb = jnp.split(proj, cuts, axis=-1)
    oa = neighbourhood_attention(qa.reshape(B, L, N_HEADS_A, HEAD_DIM), ka.reshape(B, L, N_HEADS_A, HEAD_DIM),
                                 va.reshape(B, L, N_HEADS_A, HEAD_DIM), q_norm_a, k_norm_a, rpb, rpb_meta)
    ob = window_attention(qb.reshape(B, L, N_HEADS_B, HEAD_DIM), kb.reshape(B, L, N_KV_B, HEAD_DIM),
                          vb.reshape(B, L, N_KV_B, HEAD_DIM), q_norm_b, k_norm_b, sinks)
    merged = jnp.concatenate([rmsnorm(oa, out_norm_a), rmsnorm(ob, out_norm_b)], axis=-1)
    return merged @ w_out


def moe(x, w_router, b_router, w_gate, b_gate, w_up, b_up, w_down, b_down):
    N, D = x.shape
    logits = jnp.einsum('nd,de->ne', x, w_router, preferred_element_type=jnp.float32) + b_router.astype(jnp.float32)
    top_val, top_idx = lax.top_k(logits, TOP_K)
    gates = jax.nn.softmax(top_val, axis=-1)
    A = N * TOP_K
    flat_e = top_idx.reshape(-1).astype(jnp.int32)
    flat_tok = jnp.arange(A, dtype=jnp.int32) // TOP_K
    flat_g = gates.reshape(-1)
    order = jnp.argsort(flat_e)
    sorted_e = flat_e[order]
    counts = jnp.bincount(flat_e, length=N_EXPERTS).astype(jnp.int32)
    starts = jnp.cumsum(counts) - counts
    padded = (counts + MOE_BLOCK - 1) // MOE_BLOCK * MOE_BLOCK
    pends = jnp.cumsum(padded)
    pstarts = pends - padded
    dest = pstarts[sorted_e] + (jnp.arange(A, dtype=jnp.int32) - starts[sorted_e])
    nblk = (A + N_EXPERTS * (MOE_BLOCK - 1) + MOE_BLOCK - 1) // MOE_BLOCK
    P = nblk * MOE_BLOCK
    slot_tok = jnp.full((P,), N, dtype=jnp.int32).at[dest].set(flat_tok[order])
    slot_gate = jnp.zeros((P,), jnp.float32).at[dest].set(flat_g[order])
    blk_start = jnp.arange(nblk, dtype=jnp.int32) * MOE_BLOCK
    blk_exp = jnp.minimum(jnp.sum((blk_start[:, None] >= pends[None, :]).astype(jnp.int32), axis=1), N_EXPERTS - 1)
    x_pad = jnp.concatenate([x, jnp.zeros((1, D), x.dtype)], axis=0)
    xb = x_pad[slot_tok].reshape(nblk, MOE_BLOCK, D)

    def run_block(args):
        xblk, e = args
        hg = xblk @ w_gate[e] + b_gate[e]
        hu = xblk @ w_up[e] + b_up[e]
        g = jnp.minimum(hg, SWIGLU_LIMIT)
        u = jnp.clip(hu, -SWIGLU_LIMIT, SWIGLU_LIMIT)
        act = (u + 1.0) * (g * jax.nn.sigmoid(SWIGLU_ALPHA * g))
        return act @ w_down[e] + b_down[e]

    yb = lax.map(run_block, (xb, blk_exp)).reshape(P, D)
    y = jnp.zeros((N + 1, D), jnp.float32).at[slot_tok].add(yb.astype(jnp.float32) * slot_gate[:, None])
    return y[:N].astype(x.dtype)


def setup_inputs(seed: int = 0) -> dict:
    key = jax.random.key(seed)
    ks = jax.random.split(key, 24)

    def nrm(k, shape, scale):
        return jax.random.normal(k, shape, jnp.float32) * scale

    def gain(k, shape):
        return 1.0 + 0.1 * jax.random.normal(k, shape, jnp.float32)

    return {
        'x_prompt': nrm(ks[0], (BATCH, SEQ, D_MODEL), 1.0),
        'x_sample': nrm(ks[1], (DEC_BATCH, DEC_SEQ, D_MODEL), 1.0),
        'meta_tokens': nrm(ks[2], (N_META, D_MODEL), 1.0),
        'norm_attn': gain(ks[3], (DEPTH, D_MODEL)),
        'w_in': nrm(ks[4], (DEPTH, D_MODEL, PROJ_WIDTH), D_MODEL ** -0.5),
        'q_norm_a': gain(ks[5], (DEPTH, HEAD_DIM)),
        'k_norm_a': gain(ks[6], (DEPTH, HEAD_DIM)),
        'rpb': nrm(ks[7], (DEPTH, N_HEADS_A, 2 * NA_KH - 1, 2 * NA_KW - 1), 0.5),
        'rpb_meta': nrm(ks[8], (DEPTH, N_HEADS_A, N_META), 0.5),
        'q_norm_b': gain(ks[9], (DEPTH, HEAD_DIM)),
        'k_norm_b': gain(ks[10], (DEPTH, HEAD_DIM)),
        'sinks': nrm(ks[11], (DEPTH, N_HEADS_B), 1.0),
        'out_norm_a': gain(ks[12], (DEPTH, WIDTH_A)),
        'out_norm_b': gain(ks[13], (DEPTH, WIDTH_B)),
        'w_out': nrm(ks[14], (DEPTH, MIX_WIDTH, D_MODEL), MIX_WIDTH ** -0.5),
        'norm_mlp': gain(ks[15], (DEPTH, D_MODEL)),
        'w_router': nrm(ks[16], (DEPTH, D_MODEL, N_EXPERTS), D_MODEL ** -0.5),
        'b_router': nrm(ks[17], (DEPTH, N_EXPERTS), 0.01),
        'w_gate': nrm(ks[18], (DEPTH, N_EXPERTS, D_MODEL, D_FF), D_MODEL ** -0.5),
        'b_gate': nrm(ks[19], (DEPTH, N_EXPERTS, D_FF), 0.02),
        'w_up': nrm(ks[20], (DEPTH, N_EXPERTS, D_MODEL, D_FF), D_MODEL ** -0.5),
        'b_up': nrm(ks[21], (DEPTH, N_EXPERTS, D_FF), 0.02),
        'w_down': nrm(ks[22], (DEPTH, N_EXPERTS, D_FF, D_MODEL), D_FF ** -0.5),
        'b_down': nrm(ks[23], (DEPTH, N_EXPERTS, D_MODEL), 0.02),
    }


def reference(x_prompt, x_sample, meta_tokens, norm_attn, w_in, q_norm_a, k_norm_a, rpb, rpb_meta,
              q_norm_b, k_norm_b, sinks, out_norm_a, out_norm_b, w_out, norm_mlp, w_router, b_router,
              w_gate, b_gate, w_up, b_up, w_down, b_down):
    def prepend(x):
        meta = jnp.broadcast_to(meta_tokens.astype(x.dtype)[None], (x.shape[0], N_META, x.shape[2]))
        return jnp.concatenate([meta, x], axis=1)

    xp = prepend(x_prompt)
    xs = prepend(x_sample)
    bp, lp, d = xp.shape
    bs, ls, _ = xs.shape
    for l in range(DEPTH):
        mix_args = (norm_attn[l], w_in[l], q_norm_a[l], k_norm_a[l], rpb[l], rpb_meta[l], q_norm_b[l],
                    k_norm_b[l], sinks[l], out_norm_a[l], out_norm_b[l], w_out[l])
        xp = xp + token_mixing(xp, *mix_args)
        xs = xs + token_mixing(xs, *mix_args)
        tokens = jnp.concatenate([xp.reshape(bp * lp, d), xs.reshape(bs * ls, d)], axis=0)
        tokens = tokens + moe(rmsnorm(tokens, norm_mlp[l]), w_router[l], b_router[l], w_gate[l], b_gate[l],
                              w_up[l], b_up[l], w_down[l], b_down[l])
        xp = tokens[:bp * lp].reshape(bp, lp, d)
        xs = tokens[bp * lp:].reshape(bs, ls, d)
    y_prompt = xp[:, N_META:]
    y_sample = xs[:, N_META:]
    return (y_prompt, y_sample)
```

```python
import functools

import jax
import jax.numpy as jnp
import numpy as np
from jax import lax
from jax.experimental import pallas as pl
from jax.experimental.pallas import tpu as pltpu

F32 = jnp.float32
BF16 = jnp.bfloat16
U32 = jnp.uint32
I32 = jnp.int32

HEAD_DIM = 64
N_HEADS_A = 16
N_HEADS_B = 16
N_KV_B = 4
WIDTH_A = N_HEADS_A * HEAD_DIM
WIDTH_B = N_HEADS_B * HEAD_DIM
KV_WIDTH_B = N_KV_B * HEAD_DIM
GRID_W = 64
NA_KH = 8
NA_KW = 16
WINDOW = 128
WBLOCK = 128
ROT_DIM = HEAD_DIM // 4
ROPE_THETA = 500000.0
N_META = 16
N_EXPERTS = 32
TOP_K = 4
SWIGLU_LIMIT = 7.0
SWIGLU_ALPHA = 1.702
NORM_EPS = 1e-5
NEG_INF = -1e30

LANES = 128
META_PAD = LANES
VMEM_LIMIT = 56 * 1024 * 1024

PROJ_TN = 512
SEG_QA, SEG_KA, SEG_VA, SEG_QB = 0, WIDTH_A, 2 * WIDTH_A, 3 * WIDTH_A
SEG_KB = 3 * WIDTH_A + WIDTH_B
SEG_VB = SEG_KB + 2 * KV_WIDTH_B
PROJ_COLS = SEG_VB + 2 * KV_WIDTH_B


def _cparams(sem):
    return pltpu.CompilerParams(dimension_semantics=sem, vmem_limit_bytes=VMEM_LIMIT)


def _proj_kernel(x_ref, g_ref, w_ref, gcol_ref, bd_ref, c_ref, s1_ref, s2_ref, o_ref, h_ref):
    j = pl.program_id(1)

    @pl.when(j == 0)
    def _():
        x = x_ref[...]
        ms = jnp.mean(x * x, axis=-1, keepdims=True)
        h_ref[...] = (x * lax.rsqrt(ms + NORM_EPS) * g_ref[...]).astype(BF16)

    acc = jnp.dot(h_ref[...], w_ref[...], preferred_element_type=F32)
    is_v = jnp.logical_or(j == SEG_VA // PROJ_TN, j == SEG_VA // PROJ_TN + 1)
    is_v = jnp.logical_or(is_v, j == SEG_VB // PROJ_TN)
    is_rope = jnp.logical_and(j >= SEG_QB // PROJ_TN, j < SEG_VB // PROJ_TN)

    @pl.when(is_v)
    def _():
        o_ref[...] = acc.astype(o_ref.dtype)

    def head_normed(c):
        a = acc[:, c * LANES:(c + 1) * LANES]
        ssq = jnp.dot((a * a).astype(BF16), bd_ref[...], preferred_element_type=F32)
        return a * lax.rsqrt(ssq * (1.0 / HEAD_DIM) + NORM_EPS) * gcol_ref[:, c * LANES:(c + 1) * LANES]

    @pl.when(jnp.logical_and(jnp.logical_not(is_v), jnp.logical_not(is_rope)))
    def _():
        for c in range(PROJ_TN // LANES):
            o_ref[:, c * LANES:(c + 1) * LANES] = head_normed(c).astype(o_ref.dtype)

    @pl.when(is_rope)
    def _():
        for c in range(PROJ_TN // LANES):
            y = head_normed(c)
            up = pltpu.roll(y, LANES - ROT_DIM // 2, 1)
            dn = pltpu.roll(y, ROT_DIM // 2, 1)
            r = y * c_ref[...] + up * s1_ref[...] + dn * s2_ref[...]
            o_ref[:, c * LANES:(c + 1) * LANES] = r.astype(o_ref.dtype)


def _rope_tables(positions):
    inv = ROPE_THETA ** (-np.arange(0, ROT_DIM, 2, dtype=np.float32) / ROT_DIM)
    ang = positions.astype(np.float32)[:, None] * inv[None, :]
    cos, sin = np.cos(ang), np.sin(ang)
    half = ROT_DIM // 2
    n = positions.shape[0]
    c = np.ones((n, HEAD_DIM), np.float32)
    s1 = np.zeros((n, HEAD_DIM), np.float32)
    s2 = np.zeros((n, HEAD_DIM), np.float32)
    c[:, :half] = cos
    c[:, half:ROT_DIM] = cos
    s1[:, :half] = -sin
    s2[:, half:ROT_DIM] = sin
    rep = LANES // HEAD_DIM
    return tuple(jnp.asarray(np.tile(t, (1, rep))) for t in (c, s1, s2))


def _proj(x2d, seq_len, positions, g_attn, w_cat, gcol, bd):
    n, d = x2d.shape
    tm = min(1024, seq_len)
    assert seq_len % tm == 0 and n % tm == 0
    per_seq = seq_len // tm
    c, s1, s2 = _rope_tables(positions)
    grid = (n // tm, PROJ_COLS // PROJ_TN)
    tab = pl.BlockSpec((tm, LANES), lambda i, j: (i % per_seq, 0))
    return pl.pallas_call(
        _proj_kernel,
        grid=grid,
        in_specs=[
            pl.BlockSpec((tm, d), lambda i, j: (i, 0)),
            pl.BlockSpec((1, d), lambda i, j: (0, 0)),
            pl.BlockSpec((d, PROJ_TN), lambda i, j: (0, j)),
            pl.BlockSpec((1, PROJ_TN), lambda i, j: (0, j)),
            pl.BlockSpec((LANES, LANES), lambda i, j: (0, 0)),
            tab, tab, tab,
        ],
        out_specs=pl.BlockSpec((tm, PROJ_TN), lambda i, j: (i, j)),
        out_shape=jax.ShapeDtypeStruct((n, PROJ_COLS), BF16),
        scratch_shapes=[pltpu.VMEM((tm, d), BF16)],
        compiler_params=_cparams(("parallel", "arbitrary")),
        name="proj",
    )(x2d, g_attn, w_cat, gcol, bd, c, s1, s2)


def _attn_a_kernel(q_ref, k_ref, v_ref, km_ref, vm_ref, bias_ref, bmeta_ref, o_ref, *, rows):
    lane = lax.broadcasted_iota(I32, (GRID_W, LANES), 1)
    first = lane < HEAD_DIM
    km = km_ref[...]
    vm = vm_ref[...]
    nt = (((1,), (1,)), ((), ()))

    def row_body(r, carry):
        r0 = jnp.clip(r - NA_KH // 2, 0, rows - NA_KH)
        var = r - r0
        q = q_ref[pl.ds(pl.multiple_of(r * GRID_W, GRID_W), GRID_W), :]
        ks = pl.multiple_of(r0 * GRID_W, GRID_W)
        kb = k_ref[pl.ds(ks, NA_KH * GRID_W), :]
        vb = v_ref[pl.ds(ks, NA_KH * GRID_W), :]
        outs = []
        for hh in range(2):
            qm = jnp.where(first if hh == 0 else jnp.logical_not(first), q, jnp.zeros_like(q))
            s = lax.dot_general(qm, kb, nt, preferred_element_type=F32) + bias_ref[hh, var]
            sm = lax.dot_general(qm, km, nt, preferred_element_type=F32) + bmeta_ref[hh]
            m = jnp.maximum(jnp.max(s, axis=-1, keepdims=True), jnp.max(sm, axis=-1, keepdims=True))
            p = jnp.exp(s - m)
            pm = jnp.exp(sm - m)
            l = jnp.sum(p, axis=-1, keepdims=True) + jnp.sum(pm, axis=-1, keepdims=True)
            o = (jnp.dot(p.astype(BF16), vb, preferred_element_type=F32)
                 + jnp.dot(pm.astype(BF16), vm, preferred_element_type=F32))
            outs.append(o / l)
        o_ref[pl.ds(pl.multiple_of(r * GRID_W, GRID_W), GRID_W), :] = (
            jnp.where(first, outs[0], outs[1]).astype(o_ref.dtype))
        return carry

    lax.fori_loop(0, rows, row_body, 0)


def _attn_a(proj, meta_proj, bias_tab, bias_meta, batch, seq_len):
    rows = seq_len // GRID_W
    assert rows >= NA_KH
    n = batch * seq_len
    pairs = WIDTH_A // LANES
    kernel = functools.partial(_attn_a_kernel, rows=rows)
    return pl.pallas_call(
        kernel,
        grid=(batch, pairs),
        in_specs=[
            pl.BlockSpec((seq_len, LANES), lambda b, p: (b, SEG_QA // LANES + p)),
            pl.BlockSpec((seq_len, LANES), lambda b, p: (b, SEG_KA // LANES + p)),
            pl.BlockSpec((seq_len, LANES), lambda b, p: (b, SEG_VA // LANES + p)),
            pl.BlockSpec((META_PAD, LANES), lambda b, p: (0, SEG_KA // LANES + p)),
            pl.BlockSpec((META_PAD, LANES), lambda b, p: (0, SEG_VA // LANES + p)),
            pl.BlockSpec((2, NA_KH, GRID_W, NA_KH * GRID_W), lambda b, p: (p, 0, 0, 0)),
            pl.BlockSpec((2, 1, META_PAD), lambda b, p: (p, 0, 0)),
        ],
        out_specs=pl.BlockSpec((seq_len, LANES), lambda b, p: (b, p)),
        out_shape=jax.ShapeDtypeStruct((n, WIDTH_A), BF16),
        compiler_params=_cparams(("parallel", "parallel")),
        name="attn_a",
    )(proj, proj, proj, meta_proj, meta_proj, bias_tab, bias_meta)


def _bias_tables_a(rpb, rpb_meta):
    var = np.arange(NA_KH)[:, None]
    jj = np.arange(NA_KH)[None, :]
    ridx = jj - var + NA_KH - 1
    cq = np.arange(GRID_W)
    col_start = np.clip(cq - NA_KW // 2, 0, GRID_W - NA_KW)
    col_valid = (cq[None, :] >= col_start[:, None]) & (cq[None, :] < col_start[:, None] + NA_KW)
    cidx = np.clip(cq[None, :] - cq[:, None], -(NA_KW - 1), NA_KW - 1) + NA_KW - 1
    tab = rpb.astype(F32)[:, ridx[:, :, None, None], cidx[None, None, :, :]]
    tab = jnp.where(col_valid[None, None, None], tab, NEG_INF)
    tab = jnp.transpose(tab, (0, 1, 3, 2, 4)).reshape(N_HEADS_A, NA_KH, GRID_W, NA_KH * GRID_W)
    bm = jnp.full((N_HEADS_A, 1, META_PAD), NEG_INF, F32)
    bm = bm.at[:, 0, :N_META].set(rpb_meta.astype(F32))
    return tab, bm


def _attn_b_kernel(sink_ref, q_ref, k_ref, v_ref, km_ref, vm_ref, bmeta_ref, o_ref, *, seq_len):
    kv = pl.program_id(1)
    nb = seq_len // WBLOCK
    span = 3 * WBLOCK
    lane = lax.broadcasted_iota(I32, (WBLOCK, LANES), 1)
    first = lane < HEAD_DIM
    km = km_ref[...]
    vm = vm_ref[...]
    bmeta = bmeta_ref[...]
    nt = (((1,), (1,)), ((), ()))
    qi = lax.broadcasted_iota(I32, (WBLOCK, span), 0)
    kj = lax.broadcasted_iota(I32, (WBLOCK, span), 1)
    group = N_HEADS_B // N_KV_B

    def blk_body(n, carry):
        start = jnp.clip((n - 1) * WBLOCK, 0, seq_len - span)
        start = pl.multiple_of(start, WBLOCK)
        q0 = pl.multiple_of(n * WBLOCK, WBLOCK)
        kb = k_ref[pl.ds(start, span), :]
        vb = v_ref[pl.ds(start, span), :]
        valid = jnp.abs((start + kj) - (q0 + qi)) <= WINDOW
        for c in range(group // 2):
            q = q_ref[pl.ds(q0, WBLOCK), c * LANES:(c + 1) * LANES]
            outs = []
            for hh in range(2):
                sink = sink_ref[kv * group + c * 2 + hh]
                qm = jnp.where(first if hh == 0 else jnp.logical_not(first), q, jnp.zeros_like(q))
                s = lax.dot_general(qm, kb, nt, preferred_element_type=F32)
                s = jnp.where(valid, s, NEG_INF)
                sm = lax.dot_general(qm, km, nt, preferred_element_type=F32) + bmeta
                m = jnp.maximum(jnp.max(s, axis=-1, keepdims=True), jnp.max(sm, axis=-1, keepdims=True))
                m = jnp.maximum(m, sink)
                p = jnp.exp(s - m)
                pm = jnp.exp(sm - m)
                l = (jnp.sum(p, axis=-1, keepdims=True) + jnp.sum(pm, axis=-1, keepdims=True)
                     + jnp.exp(sink - m))
                o = (jnp.dot(p.astype(BF16), vb, preferred_element_type=F32)
                     + jnp.dot(pm.astype(BF16), vm, preferred_element_type=F32))
                outs.append(o / l)
            o_ref[pl.ds(q0, WBLOCK), c * LANES:(c + 1) * LANES] = (
                jnp.where(first, outs[0], outs[1]).astype(o_ref.dtype))
        return carry

    lax.fori_loop(0, nb, blk_body, 0)


def _attn_b(proj, meta_proj, sinks, batch, seq_len):
    assert seq_len % WBLOCK == 0 and seq_len >= 3 * WBLOCK
    n = batch * seq_len
    qw = WIDTH_B // N_KV_B
    bmeta = jnp.where(jnp.arange(META_PAD) < N_META, 0.0, NEG_INF).astype(F32)[None, :]
    kernel = functools.partial(_attn_b_kernel, seq_len=seq_len)
    grid_spec = pltpu.PrefetchScalarGridSpec(
        num_scalar_prefetch=1,
        grid=(batch, N_KV_B),
        in_specs=[
            pl.BlockSpec((seq_len, qw), lambda b, k, s: (b, SEG_QB // qw + k)),
            pl.BlockSpec((seq_len, LANES), lambda b, k, s: (b, SEG_KB // LANES + k)),
            pl.BlockSpec((seq_len, LANES), lambda b, k, s: (b, SEG_VB // LANES + k)),
            pl.BlockSpec((META_PAD, LANES), lambda b, k, s: (0, SEG_KB // LANES + k)),
            pl.BlockSpec((META_PAD, LANES), lambda b, k, s: (0, SEG_VB // LANES + k)),
            pl.BlockSpec((1, META_PAD), lambda b, k, s: (0, 0)),
        ],
        out_specs=pl.BlockSpec((seq_len, qw), lambda b, k, s: (b, k)),
    )
    return pl.pallas_call(
        kernel,
        grid_spec=grid_spec,
        out_shape=jax.ShapeDtypeStruct((n, WIDTH_B), BF16),
        compiler_params=_cparams(("parallel", "parallel")),
        name="attn_b",
    )(sinks.astype(F32), proj, proj, proj, meta_proj, meta_proj, bmeta)


def _pack_halves(a):
    w = a.shape[1] // 2
    lo = pltpu.bitcast(a[:, :w].astype(BF16).astype(F32), U32)
    hi = pltpu.bitcast(a[:, w:].astype(BF16).astype(F32), U32)
    return (lo >> 16) | (hi & jnp.uint32(0xFFFF0000))


def _unpack_halves(u):
    lo = pltpu.bitcast(u << 16, F32)
    hi = pltpu.bitcast(u & jnp.uint32(0xFFFF0000), F32)
    return lo, hi


def _post_kernel(oa_ref, ob_ref, x_ref, ga_ref, gb_ref, wo_ref, gm_ref, wrh_ref, wrl_ref, br_ref,
                 tri_ref, cnt0_ref, x1_ref, xn_ref, ei_ref, gate_ref, cnt_ref, run_ref):
    i = pl.program_id(0)

    @pl.when(i == 0)
    def _():
        run_ref[...] = cnt0_ref[...]

    def normed(ref, g_ref):
        a = ref[...].astype(F32)
        ms = jnp.mean(a * a, axis=-1, keepdims=True)
        return (a * lax.rsqrt(ms + NORM_EPS) * g_ref[...]).astype(BF16)

    wa = oa_ref.shape[1]
    mix = (jnp.dot(normed(oa_ref, ga_ref), wo_ref[:wa, :], preferred_element_type=F32)
           + jnp.dot(normed(ob_ref, gb_ref), wo_ref[wa:, :], preferred_element_type=F32))
    x1 = x_ref[...] + mix
    x1_ref[...] = x1
    ms = jnp.mean(x1 * x1, axis=-1, keepdims=True)
    xn = x1 * lax.rsqrt(ms + NORM_EPS) * gm_ref[...]
    xn_ref[...] = _pack_halves(xn)

    xh = xn.astype(BF16)
    xl = (xn - xh.astype(F32)).astype(BF16)
    logits = (jnp.dot(xh, wrh_ref[...], preferred_element_type=F32)
              + jnp.dot(xl, wrh_ref[...], preferred_element_type=F32)
              + jnp.dot(xh, wrl_ref[...], preferred_element_type=F32)) + br_ref[...]

    tm = logits.shape[0]
    lane = lax.broadcasted_iota(I32, (tm, LANES), 1)
    lanef = lane.astype(F32)
    work = logits
    vals, idxs = [], []
    chosen = jnp.zeros((tm, LANES), F32)
    for _ in range(TOP_K):
        mk = jnp.max(work, axis=-1, keepdims=True)
        ik = jnp.min(jnp.where(work == mk, lanef, float(LANES)), axis=-1, keepdims=True).astype(I32)
        hit = lane == ik
        work = jnp.where(hit, -jnp.inf, work)
        chosen = jnp.where(hit, 1.0, chosen)
        vals.append(mk)
        idxs.append(ik)
    ex = [jnp.exp(v - vals[0]) for v in vals]
    den = ex[0] + ex[1] + ex[2] + ex[3]

    prefix = jnp.dot(tri_ref[...], chosen.astype(BF16), preferred_element_type=F32) + run_ref[...]
    run_ref[...] = run_ref[...] + jnp.sum(chosen, axis=0, keepdims=True)
    cnt_ref[...] = run_ref[...]

    ei = jnp.zeros((tm, LANES), I32)
    gates = jnp.zeros((tm, LANES), F32)
    for k in range(TOP_K):
        rank = jnp.sum(jnp.where(lane == idxs[k], prefix, 0.0), axis=-1, keepdims=True).astype(I32)
        ei = jnp.where(lane == k, idxs[k], ei)
        ei = jnp.where(lane == TOP_K + k, rank, ei)
        gates = jnp.where(lane == k, ex[k] / den, gates)
    ei_ref[...] = ei
    gate_ref[...] = gates


def _post(oa, ob, x2d, ga, gb, wo, gm, wrh, wrl, br, cnt0):
    n, d = x2d.shape
    tm = 256
    assert n % tm == 0
    tri = jnp.asarray(np.tril(np.ones((tm, tm), np.float32), -1), BF16)
    row = lambda i: (i, 0)
    fixed = lambda i: (0, 0)
    return pl.pallas_call(
        _post_kernel,
        grid=(n // tm,),
        in_specs=[
            pl.BlockSpec((tm, oa.shape[1]), row),
            pl.BlockSpec((tm, ob.shape[1]), row),
            pl.BlockSpec((tm, d), row),
            pl.BlockSpec((1, oa.shape[1]), fixed),
            pl.BlockSpec((1, ob.shape[1]), fixed),
            pl.BlockSpec(wo.shape, fixed),
            pl.BlockSpec((1, d), fixed),
            pl.BlockSpec((d, LANES), fixed),
            pl.BlockSpec((d, LANES), fixed),
            pl.BlockSpec((1, LANES), fixed),
            pl.BlockSpec((tm, tm), fixed),
            pl.BlockSpec((1, LANES), fixed),
        ],
        out_specs=[
            pl.BlockSpec((tm, d), row),
            pl.BlockSpec((tm, d // 2), row),
            pl.BlockSpec((tm, LANES), row),
            pl.BlockSpec((tm, LANES), row),
            pl.BlockSpec((1, LANES), fixed),
        ],
        out_shape=[
            jax.ShapeDtypeStruct((n, d), F32),
            jax.ShapeDtypeStruct((n, d // 2), U32),
            jax.ShapeDtypeStruct((n, LANES), I32),
            jax.ShapeDtypeStruct((n, LANES), F32),
            jax.ShapeDtypeStruct((1, LANES), F32),
        ],
        scratch_shapes=[pltpu.VMEM((1, LANES), F32)],
        compiler_params=_cparams(("arbitrary",)),
        name="post",
    )(oa, ob, x2d, ga, gb, wo, gm, wrh, wrl, br, tri, cnt0)


def _dispatch_kernel(slot_ref, xn_ref, xs_in_ref, xs_ref, sems, *, tk):
    del xs_in_ref

    def row_copy(t, k):
        dst = slot_ref[0, 0, t * TOP_K + k]
        return pltpu.make_async_copy(xn_ref.at[pl.ds(t, 1)], xs_ref.at[pl.ds(dst, 1)], sems.at[k])

    def issue(t, carry):
        for k in range(TOP_K):
            row_copy(t, k).start()
        return carry

    lax.fori_loop(0, tk, issue, 0)
    for k in range(TOP_K):
        pltpu.make_async_copy(xn_ref, xs_ref.at[pl.ds(0, tk)], sems.at[k]).wait()


def _dispatch(slots, xn, xs):
    n, w = xn.shape
    tk = 512
    assert n % tk == 0
    slots3 = slots.reshape(n // tk, 1, tk * TOP_K)
    kernel = functools.partial(_dispatch_kernel, tk=tk)
    return pl.pallas_call(
        kernel,
        grid=(n // tk,),
        in_specs=[
            pl.BlockSpec((1, 1, tk * TOP_K), lambda i: (i, 0, 0), memory_space=pltpu.SMEM),
            pl.BlockSpec((tk, w), lambda i: (i, 0)),
            pl.BlockSpec(memory_space=pl.ANY),
        ],
        out_specs=pl.BlockSpec(memory_space=pl.ANY),
        out_shape=jax.ShapeDtypeStruct(xs.shape, xs.dtype),
        input_output_aliases={2: 0},
        scratch_shapes=[pltpu.SemaphoreType.DMA((TOP_K,))],
        compiler_params=_cparams(("arbitrary",)),
        name="dispatch",
    )(slots3, xn, xs)


def _experts_kernel(bexp_ref, bval_ref, xs_ref, wg_ref, bg_ref, wu_ref, bu_ref, wd_ref, bd_ref,
                    ys_ref, lo_ref, hi_ref, acc_ref):
    del bexp_ref
    i = pl.program_id(0)
    j = pl.program_id(1)
    nvalid = bval_ref[i]

    @pl.when(jnp.logical_and(nvalid == 0, j == 0))
    def _():
        ys_ref[...] = jnp.zeros_like(ys_ref)

    @pl.when(nvalid > 0)
    def _():
        @pl.when(j == 0)
        def _():
            u = xs_ref[...]
            rows = lax.broadcasted_iota(I32, u.shape, 0)
            u = jnp.where(rows < nvalid, u, jnp.zeros_like(u))
            lo, hi = _unpack_halves(u)
            lo_ref[...] = lo.astype(BF16)
            hi_ref[...] = hi.astype(BF16)
            acc_ref[...] = jnp.zeros_like(acc_ref)

        half = lo_ref.shape[1]
        lo = lo_ref[...]
        hi = hi_ref[...]
        hg = (jnp.dot(lo, wg_ref[0, :half, :], preferred_element_type=F32)
              + jnp.dot(hi, wg_ref[0, half:, :], preferred_element_type=F32) + bg_ref[0])
        hu = (jnp.dot(lo, wu_ref[0, :half, :], preferred_element_type=F32)
              + jnp.dot(hi, wu_ref[0, half:, :], preferred_element_type=F32) + bu_ref[0])
        g = jnp.minimum(hg, SWIGLU_LIMIT)
        u = jnp.clip(hu, -SWIGLU_LIMIT, SWIGLU_LIMIT)
        act = (u + 1.0) * (g * jax.nn.sigmoid(SWIGLU_ALPHA * g))
        acc_ref[...] += jnp.dot(act.astype(BF16), wd_ref[0], preferred_element_type=F32)

        @pl.when(j == pl.num_programs(1) - 1)
        def _():
            ys_ref[...] = _pack_halves(acc_ref[...] + bd_ref[0])


def _experts(blk_exp, blk_valid, xs, wg, bg, wu, bu, wd, bd, tm, tf):
    p, half = xs.shape
    d = 2 * half
    dff = wg.shape[2]
    nblk = p // tm
    nf = dff // tf

    def jeff(i, j, bval):
        return jnp.where(bval[i] > 0, j, nf - 1)

    grid_spec = pltpu.PrefetchScalarGridSpec(
        num_scalar_prefetch=2,
        grid=(nblk, nf),
        in_specs=[
            pl.BlockSpec((tm, half), lambda i, j, be, bv: (i, 0)),
            pl.BlockSpec((1, d, tf), lambda i, j, be, bv: (be[i], 0, jeff(i, j, bv))),
            pl.BlockSpec((1, 1, tf), lambda i, j, be, bv: (be[i], 0, jeff(i, j, bv))),
            pl.BlockSpec((1, d, tf), lambda i, j, be, bv: (be[i], 0, jeff(i, j, bv))),
            pl.BlockSpec((1, 1, tf), lambda i, j, be, bv: (be[i], 0, jeff(i, j, bv))),
            pl.BlockSpec((1, tf, d), lambda i, j, be, bv: (be[i], jeff(i, j, bv), 0)),
            pl.BlockSpec((1, 1, d), lambda i, j, be, bv: (be[i], 0, 0)),
        ],
        out_specs=pl.BlockSpec((tm, half), lambda i, j, be, bv: (i, 0)),
        scratch_shapes=[pltpu.VMEM((tm, half), BF16), pltpu.VMEM((tm, half), BF16),
                        pltpu.VMEM((tm, d), F32)],
    )
    return pl.pallas_call(
        _experts_kernel,
        grid_spec=grid_spec,
        out_shape=jax.ShapeDtypeStruct((p, half), U32),
        compiler_params=_cparams(("arbitrary", "arbitrary")),
        name="experts",
    )(blk_exp, blk_valid, xs, wg, bg, wu, bu, wd, bd)


def _combine_kernel(slot_ref, x1_ref, gate_ref, ys_ref, o_ref, buf_ref, sems, *, tk):
    def row_copy(t, k):
        src = slot_ref[0, 0, t * TOP_K + k]
        return pltpu.make_async_copy(ys_ref.at[pl.ds(src, 1)], buf_ref.at[k, pl.ds(t, 1)], sems.at[k])

    def issue(t, carry):
        for k in range(TOP_K):
            row_copy(t, k).start()
        return carry

    lax.fori_loop(0, tk, issue, 0)
    half = buf_ref.shape[2]
    gates = gate_ref[...]
    acc_lo = x1_ref[:, :half]
    acc_hi = x1_ref[:, half:]
    for k in range(TOP_K):
        pltpu.make_async_copy(ys_ref.at[pl.ds(0, tk)], buf_ref.at[k], sems.at[k]).wait()
        lo, hi = _unpack_halves(buf_ref[k])
        g = gates[:, k:k + 1]
        acc_lo = acc_lo + g * lo
        acc_hi = acc_hi + g * hi
    o_ref[:, :half] = acc_lo
    o_ref[:, half:] = acc_hi


def _combine(slots, x1, gates, ys):
    n, d = x1.shape
    tk = 256
    assert n % tk == 0
    slots3 = slots.reshape(n // tk, 1, tk * TOP_K)
    kernel = functools.partial(_combine_kernel, tk=tk)
    return pl.pallas_call(
        kernel,
        grid=(n // tk,),
        in_specs=[
            pl.BlockSpec((1, 1, tk * TOP_K), lambda i: (i, 0, 0), memory_space=pltpu.SMEM),
            pl.BlockSpec((tk, d), lambda i: (i, 0)),
            pl.BlockSpec((tk, LANES), lambda i: (i, 0)),
            pl.BlockSpec(memory_space=pl.ANY),
        ],
        out_specs=pl.BlockSpec((tk, d), lambda i: (i, 0)),
        out_shape=jax.ShapeDtypeStruct((n, d), F32),
        scratch_shapes=[pltpu.VMEM((TOP_K, tk, d // 2), U32), pltpu.SemaphoreType.DMA((TOP_K,))],
        compiler_params=_cparams(("arbitrary",)),
        name="combine",
    )(slots3, x1, gates, ys)


MOE_TM = 512
MOE_TF = 512


def _dup_kv_heads(w):
    d = w.shape[0]
    w = w.reshape(d, N_KV_B, 1, HEAD_DIM)
    return jnp.broadcast_to(w, (d, N_KV_B, 2, HEAD_DIM)).reshape(d, 2 * KV_WIDTH_B)


def kernel(x_prompt, x_sample, meta_tokens, norm_attn, w_in, q_norm_a, k_norm_a, rpb, rpb_meta, q_norm_b, k_norm_b, sinks, out_norm_a, out_norm_b, w_out, norm_mlp, w_router, b_router, w_gate, b_gate, w_up, b_up, w_down, b_down):
    depth = norm_attn.shape[0]
    assert depth == 1
    d = x_prompt.shape[-1]
    groups = [(x_prompt.reshape(-1, d), x_prompt.shape[0], x_prompt.shape[1]),
              (x_sample.reshape(-1, d), x_sample.shape[0], x_sample.shape[1])]

    wi = w_in[0]
    w_cat = jnp.concatenate([wi[:, :SEG_KB], _dup_kv_heads(wi[:, SEG_KB:SEG_KB + KV_WIDTH_B]),
                             _dup_kv_heads(wi[:, SEG_KB + KV_WIDTH_B:])], axis=1).astype(BF16)
    scale = HEAD_DIM ** -0.5
    gcol = jnp.concatenate([
        jnp.tile(q_norm_a[0].astype(F32) * scale, N_HEADS_A), jnp.tile(k_norm_a[0].astype(F32), N_HEADS_A),
        jnp.ones((WIDTH_A,), F32),
        jnp.tile(q_norm_b[0].astype(F32) * scale, N_HEADS_B), jnp.tile(k_norm_b[0].astype(F32), 2 * N_KV_B),
        jnp.ones((2 * KV_WIDTH_B,), F32)])[None, :]
    bd = jnp.asarray(np.kron(np.eye(LANES // HEAD_DIM, dtype=np.float32),
                             np.ones((HEAD_DIM, HEAD_DIM), np.float32)), BF16)
    g_attn = norm_attn[0].astype(F32)[None, :]
    bias_tab, bias_meta = _bias_tables_a(rpb[0], rpb_meta[0])
    wo = w_out[0].astype(BF16)
    ga = out_norm_a[0].astype(F32)[None, :]
    gb = out_norm_b[0].astype(F32)[None, :]
    gm = norm_mlp[0].astype(F32)[None, :]
    wr = jnp.zeros((d, LANES), F32).at[:, :N_EXPERTS].set(w_router[0].astype(F32))
    wrh = wr.astype(BF16)
    wrl = (wr - wrh.astype(F32)).astype(BF16)
    br = jnp.full((1, LANES), NEG_INF, F32).at[0, :N_EXPERTS].set(b_router[0].astype(F32))
    wg = w_gate[0].astype(BF16)
    wu = w_up[0].astype(BF16)
    wd = w_down[0].astype(BF16)
    bg = b_gate[0].astype(F32)[:, None, :]
    bu = b_up[0].astype(F32)[:, None, :]
    bdn = b_down[0].astype(F32)[:, None, :]

    meta_x = jnp.zeros((META_PAD, d), F32).at[:N_META].set(meta_tokens.astype(F32))
    meta_pos = np.minimum(np.arange(META_PAD), N_META - 1)
    meta_proj = _proj(meta_x, META_PAD, meta_pos, g_attn, w_cat, gcol, bd)
    meta_proj = jnp.where(jnp.arange(META_PAD)[:, None] < N_META, meta_proj, jnp.zeros_like(meta_proj))

    cnt = jnp.zeros((1, LANES), F32)
    staged = []
    for x2d, batch, seq_len in groups:
        pos = N_META + np.arange(seq_len)
        proj = _proj(x2d, seq_len, pos, g_attn, w_cat, gcol, bd)
        oa = _attn_a(proj, meta_proj, bias_tab, bias_meta, batch, seq_len)
        ob = _attn_b(proj, meta_proj, sinks[0], batch, seq_len)
        x1, xn, ei, gates, cnt = _post(oa, ob, x2d, ga, gb, wo, gm, wrh, wrl, br, cnt)
        staged.append((x1, xn, ei, gates))

    total = sum(g[0].shape[0] for g in groups) * TOP_K
    nblk = (total + N_EXPERTS * (MOE_TM - 1) + MOE_TM - 1) // MOE_TM
    counts = cnt[0, :N_EXPERTS].astype(I32)
    padded = (counts + MOE_TM - 1) // MOE_TM * MOE_TM
    pends = jnp.cumsum(padded)
    pstarts = pends - padded
    blk_start = jnp.arange(nblk, dtype=I32) * MOE_TM
    blk_exp = jnp.minimum(jnp.sum((blk_start[:, None] >= pends[None, :]).astype(I32), axis=1), N_EXPERTS - 1)
    blk_valid = jnp.clip(counts[blk_exp] - (blk_start - pstarts[blk_exp]), 0, MOE_TM)
    blk_valid = jnp.where(blk_start < pends[-1], blk_valid, 0).astype(I32)
    last_used = jnp.maximum(pends[-1] // MOE_TM - 1, 0)
    blk_exp = jnp.where(blk_start < pends[-1], blk_exp, blk_exp[last_used]).astype(I32)

    slots = [pstarts[ei[:, :TOP_K]] + ei[:, TOP_K:2 * TOP_K] for (_, _, ei, _) in staged]

    xs = jnp.zeros((nblk * MOE_TM, d // 2), U32)
    for (x1, xn, ei, gates), sl in zip(staged, slots):
        xs = _dispatch(sl, xn, xs)
    ys = _experts(blk_exp, blk_valid, xs, wg, bg, wu, bu, wd, bdn, MOE_TM, MOE_TF)

    outs = []
    for (x1, xn, ei, gates), sl, (x2d, batch, seq_len) in zip(staged, slots, groups):
        y = _combine(sl, x1, gates, ys)
        outs.append(y.reshape(batch, seq_len, d))
    return tuple(outs)
```

```python
import functools

import jax
import jax.numpy as jnp
import numpy as np
from jax import lax
from jax.experimental import pallas as pl
from jax.experimental.pallas import tpu as pltpu

F32 = jnp.float32
BF16 = jnp.bfloat16
U32 = jnp.uint32
I32 = jnp.int32

HEAD_DIM = 64
N_HEADS_A = 16
N_HEADS_B = 16
N_KV_B = 4
WIDTH_A = N_HEADS_A * HEAD_DIM
WIDTH_B = N_HEADS_B * HEAD_DIM
KV_WIDTH_B = N_KV_B * HEAD_DIM
GRID_W = 64
NA_KH = 8
NA_KW = 16
WINDOW = 128
WBLOCK = 128
ROT_DIM = HEAD_DIM // 4
ROPE_THETA = 500000.0
N_META = 16
N_EXPERTS = 32
TOP_K = 4
SWIGLU_LIMIT = 7.0
SWIGLU_ALPHA = 1.702
NORM_EPS = 1e-5
NEG_INF = -1e30

LANES = 128
META_PAD = LANES
VMEM_LIMIT = 56 * 1024 * 1024

PROJ_TN = 512
SEG_QA, SEG_KA, SEG_VA, SEG_QB = 0, WIDTH_A, 2 * WIDTH_A, 3 * WIDTH_A
SEG_KB = 3 * WIDTH_A + WIDTH_B
SEG_VB = SEG_KB + 2 * KV_WIDTH_B
PROJ_COLS = SEG_VB + 2 * KV_WIDTH_B


def _cparams(sem):
    return pltpu.CompilerParams(dimension_semantics=sem, vmem_limit_bytes=VMEM_LIMIT)


def _proj_kernel(x_ref, g_ref, w_ref, gcol_ref, bd_ref, c_ref, s1_ref, s2_ref, o_ref, h_ref):
    j = pl.program_id(1)

    @pl.when(j == 0)
    def _():
        x = x_ref[...]
        ms = jnp.mean(x * x, axis=-1, keepdims=True)
        h_ref[...] = (x * lax.rsqrt(ms + NORM_EPS) * g_ref[...]).astype(BF16)

    acc = jnp.dot(h_ref[...], w_ref[...], preferred_element_type=F32)
    is_v = jnp.logical_or(j == SEG_VA // PROJ_TN, j == SEG_VA // PROJ_TN + 1)
    is_v = jnp.logical_or(is_v, j == SEG_VB // PROJ_TN)
    is_rope = jnp.logical_and(j >= SEG_QB // PROJ_TN, j < SEG_VB // PROJ_TN)

    @pl.when(is_v)
    def _():
        o_ref[...] = acc.astype(o_ref.dtype)

    def head_normed(c):
        a = acc[:, c * LANES:(c + 1) * LANES]
        ssq = jnp.dot((a * a).astype(BF16), bd_ref[...], preferred_element_type=F32)
        return a * lax.rsqrt(ssq * (1.0 / HEAD_DIM) + NORM_EPS) * gcol_ref[:, c * LANES:(c + 1) * LANES]

    @pl.when(jnp.logical_and(jnp.logical_not(is_v), jnp.logical_not(is_rope)))
    def _():
        for c in range(PROJ_TN // LANES):
            o_ref[:, c * LANES:(c + 1) * LANES] = head_normed(c).astype(o_ref.dtype)

    @pl.when(is_rope)
    def _():
        for c in range(PROJ_TN // LANES):
            y = head_normed(c)
            up = pltpu.roll(y, LANES - ROT_DIM // 2, 1)
            dn = pltpu.roll(y, ROT_DIM // 2, 1)
            r = y * c_ref[...] + up * s1_ref[...] + dn * s2_ref[...]
            o_ref[:, c * LANES:(c + 1) * LANES] = r.astype(o_ref.dtype)


def _rope_tables(positions):
    inv = ROPE_THETA ** (-np.arange(0, ROT_DIM, 2, dtype=np.float32) / ROT_DIM)
    ang = positions.astype(np.float32)[:, None] * inv[None, :]
    cos, sin = np.cos(ang), np.sin(ang)
    half = ROT_DIM // 2
    n = positions.shape[0]
    c = np.ones((n, HEAD_DIM), np.float32)
    s1 = np.zeros((n, HEAD_DIM), np.float32)
    s2 = np.zeros((n, HEAD_DIM), np.float32)
    c[:, :half] = cos
    c[:, half:ROT_DIM] = cos
    s1[:, :half] = -sin
    s2[:, half:ROT_DIM] = sin
    rep = LANES // HEAD_DIM
    return tuple(jnp.asarray(np.tile(t, (1, rep))) for t in (c, s1, s2))


def _proj(x2d, seq_len, positions, g_attn, w_cat, gcol, bd):
    n, d = x2d.shape
    tm = min(1024, seq_len)
    assert seq_len % tm == 0 and n % tm == 0
    per_seq = seq_len // tm
    c, s1, s2 = _rope_tables(positions)
    grid = (n // tm, PROJ_COLS // PROJ_TN)
    tab = pl.BlockSpec((tm, LANES), lambda i, j: (i % per_seq, 0))
    return pl.pallas_call(
        _proj_kernel,
        grid=grid,
        in_specs=[
            pl.BlockSpec((tm, d), lambda i, j: (i, 0)),
            pl.BlockSpec((1, d), lambda i, j: (0, 0)),
            pl.BlockSpec((d, PROJ_TN), lambda i, j: (0, j)),
            pl.BlockSpec((1, PROJ_TN), lambda i, j: (0, j)),
            pl.BlockSpec((LANES, LANES), lambda i, j: (0, 0)),
            tab, tab, tab,
        ],
        out_specs=pl.BlockSpec((tm, PROJ_TN), lambda i, j: (i, j)),
        out_shape=jax.ShapeDtypeStruct((n, PROJ_COLS), BF16),
        scratch_shapes=[pltpu.VMEM((tm, d), BF16)],
        compiler_params=_cparams(("parallel", "arbitrary")),
        name="proj",
    )(x2d, g_attn, w_cat, gcol, bd, c, s1, s2)


ATTN_A_UNROLL = 4


def _attn_a_kernel(q_ref, k_ref, v_ref, km_ref, vm_ref, bias_ref, bmeta_ref, o_ref, *, rows):
    lane = lax.broadcasted_iota(I32, (GRID_W, LANES), 1)
    first = lane < HEAD_DIM
    km = km_ref[...]
    vm = vm_ref[...]
    bmeta = bmeta_ref[0]
    nt = (((1,), (1,)), ((), ()))

    def row_body(r, carry):
        r0 = jnp.clip(r - NA_KH // 2, 0, rows - NA_KH)
        var = r - r0
        q = q_ref[pl.ds(pl.multiple_of(r * GRID_W, GRID_W), GRID_W), :]
        zero = jnp.zeros_like(q)
        qs = jnp.concatenate([jnp.where(first, q, zero), jnp.where(first, zero, q)], axis=0)
        ks = pl.multiple_of(r0 * GRID_W, GRID_W)
        kb = k_ref[pl.ds(ks, NA_KH * GRID_W), :]
        vb = v_ref[pl.ds(ks, NA_KH * GRID_W), :]
        s = lax.dot_general(qs, kb, nt, preferred_element_type=F32) + bias_ref[0, var]
        sm = lax.dot_general(qs, km, nt, preferred_element_type=F32) + bmeta
        m = jnp.maximum(jnp.max(s, axis=-1, keepdims=True), jnp.max(sm, axis=-1, keepdims=True))
        p = jnp.exp(s - m)
        pm = jnp.exp(sm - m)
        l = jnp.sum(p, axis=-1, keepdims=True) + jnp.sum(pm, axis=-1, keepdims=True)
        o = (jnp.dot(p.astype(BF16), vb, preferred_element_type=F32)
             + jnp.dot(pm.astype(BF16), vm, preferred_element_type=F32)) / l
        o_ref[pl.ds(pl.multiple_of(r * GRID_W, GRID_W), GRID_W), :] = (
            jnp.where(first, o[:GRID_W], o[GRID_W:]).astype(o_ref.dtype))
        return carry

    lax.fori_loop(0, rows, row_body, 0, unroll=ATTN_A_UNROLL)


def _attn_a(proj, meta_proj, bias_tab, bias_meta, batch, seq_len):
    rows = seq_len // GRID_W
    assert rows >= NA_KH and rows % ATTN_A_UNROLL == 0
    n = batch * seq_len
    pairs = WIDTH_A // LANES
    kernel = functools.partial(_attn_a_kernel, rows=rows)
    return pl.pallas_call(
        kernel,
        grid=(batch, pairs),
        in_specs=[
            pl.BlockSpec((seq_len, LANES), lambda b, p: (b, SEG_QA // LANES + p)),
            pl.BlockSpec((seq_len, LANES), lambda b, p: (b, SEG_KA // LANES + p)),
            pl.BlockSpec((seq_len, LANES), lambda b, p: (b, SEG_VA // LANES + p)),
            pl.BlockSpec((META_PAD, LANES), lambda b, p: (0, SEG_KA // LANES + p)),
            pl.BlockSpec((META_PAD, LANES), lambda b, p: (0, SEG_VA // LANES + p)),
            pl.BlockSpec((1, NA_KH, 2 * GRID_W, NA_KH * GRID_W), lambda b, p: (p, 0, 0, 0)),
            pl.BlockSpec((1, 2 * GRID_W, META_PAD), lambda b, p: (p, 0, 0)),
        ],
        out_specs=pl.BlockSpec((seq_len, LANES), lambda b, p: (b, p)),
        out_shape=jax.ShapeDtypeStruct((n, WIDTH_A), BF16),
        compiler_params=_cparams(("parallel", "parallel")),
        name="attn_a",
    )(proj, proj, proj, meta_proj, meta_proj, bias_tab, bias_meta)


def _bias_tables_a(rpb, rpb_meta):
    n_dr, n_dc = 2 * NA_KH - 1, 2 * NA_KW - 1
    var = np.arange(NA_KH)[:, None]
    jj = np.arange(NA_KH)[None, :]
    ridx = jj - var + NA_KH - 1
    cq = np.arange(GRID_W)
    col_start = np.clip(cq - NA_KW // 2, 0, GRID_W - NA_KW)
    col_valid = (cq[None, :] >= col_start[:, None]) & (cq[None, :] < col_start[:, None] + NA_KW)
    cidx = np.clip(cq[None, :] - cq[:, None], -(NA_KW - 1), NA_KW - 1) + NA_KW - 1
    sel_r = (ridx[None, :, :] == np.arange(n_dr)[:, None, None]).astype(np.float32)
    sel_c = (cidx[None, :, :] == np.arange(n_dc)[:, None, None]).astype(np.float32)
    hi = lax.Precision.HIGHEST
    t1 = jnp.einsum('hab,bcw->hacw', rpb.astype(F32), jnp.asarray(sel_c), precision=hi)
    tab = jnp.einsum('avj,hacw->hvcjw', jnp.asarray(sel_r), t1, precision=hi)
    tab = jnp.where(col_valid[None, None, :, None, :], tab, NEG_INF)
    tab = tab.reshape(N_HEADS_A // 2, 2, NA_KH, GRID_W, NA_KH * GRID_W)
    tab = jnp.transpose(tab, (0, 2, 1, 3, 4)).reshape(N_HEADS_A // 2, NA_KH, 2 * GRID_W, NA_KH * GRID_W)
    bm = jnp.full((N_HEADS_A, META_PAD), NEG_INF, F32).at[:, :N_META].set(rpb_meta.astype(F32))
    bm = jnp.broadcast_to(bm[:, None, :], (N_HEADS_A, GRID_W, META_PAD)).reshape(N_HEADS_A // 2, 2 * GRID_W, META_PAD)
    return tab, bm


ATTN_B_UNROLL = 2


def _attn_b_kernel(sink_ref, q_ref, k_ref, v_ref, km_ref, vm_ref, bmeta_ref, o_ref, *, seq_len):
    kv = pl.program_id(1)
    nb = seq_len // WBLOCK
    span = 3 * WBLOCK
    group = N_HEADS_B // N_KV_B
    stack = group * WBLOCK
    lane = lax.broadcasted_iota(I32, (WBLOCK, LANES), 1)
    first = lane < HEAD_DIM
    km = km_ref[...]
    vm = vm_ref[...]
    bmeta = bmeta_ref[...]
    nt = (((1,), (1,)), ((), ()))
    qi = lax.broadcasted_iota(I32, (stack, span), 0) % WBLOCK
    kj = lax.broadcasted_iota(I32, (stack, span), 1)
    rel = kj - qi
    head_of_row = lax.broadcasted_iota(I32, (stack, 1), 0) // WBLOCK
    sink = jnp.zeros((stack, 1), F32)
    for g in range(group):
        sink = jnp.where(head_of_row == g, sink_ref[kv * group + g], sink)

    def blk_body(n, carry):
        start = jnp.clip((n - 1) * WBLOCK, 0, seq_len - span)
        start = pl.multiple_of(start, WBLOCK)
        q0 = pl.multiple_of(n * WBLOCK, WBLOCK)
        kb = k_ref[pl.ds(start, span), :]
        vb = v_ref[pl.ds(start, span), :]
        parts = []
        for c in range(group // 2):
            q = q_ref[pl.ds(q0, WBLOCK), c * LANES:(c + 1) * LANES]
            zero = jnp.zeros_like(q)
            parts += [jnp.where(first, q, zero), jnp.where(first, zero, q)]
        qs = jnp.concatenate(parts, axis=0)
        s = lax.dot_general(qs, kb, nt, preferred_element_type=F32)
        s = jnp.where(jnp.abs(rel + (start - q0)) <= WINDOW, s, NEG_INF)
        sm = lax.dot_general(qs, km, nt, preferred_element_type=F32) + bmeta
        m = jnp.maximum(jnp.max(s, axis=-1, keepdims=True), jnp.max(sm, axis=-1, keepdims=True))
        m = jnp.maximum(m, sink)
        p = jnp.exp(s - m)
        pm = jnp.exp(sm - m)
        l = jnp.sum(p, axis=-1, keepdims=True) + jnp.sum(pm, axis=-1, keepdims=True) + jnp.exp(sink - m)
        o = (jnp.dot(p.astype(BF16), vb, preferred_element_type=F32)
             + jnp.dot(pm.astype(BF16), vm, preferred_element_type=F32)) / l
        for c in range(group // 2):
            base = 2 * c * WBLOCK
            o_ref[pl.ds(q0, WBLOCK), c * LANES:(c + 1) * LANES] = jnp.where(
                first, o[base:base + WBLOCK], o[base + WBLOCK:base + 2 * WBLOCK]).astype(o_ref.dtype)
        return carry

    lax.fori_loop(0, nb, blk_body, 0, unroll=ATTN_B_UNROLL)


def _attn_b(proj, meta_proj, sinks, batch, seq_len):
    assert seq_len % (WBLOCK * ATTN_B_UNROLL) == 0 and seq_len >= 3 * WBLOCK
    n = batch * seq_len
    qw = WIDTH_B // N_KV_B
    bmeta = jnp.where(jnp.arange(META_PAD) < N_META, 0.0, NEG_INF).astype(F32)[None, :]
    kernel = functools.partial(_attn_b_kernel, seq_len=seq_len)
    grid_spec = pltpu.PrefetchScalarGridSpec(
        num_scalar_prefetch=1,
        grid=(batch, N_KV_B),
        in_specs=[
            pl.BlockSpec((seq_len, qw), lambda b, k, s: (b, SEG_QB // qw + k)),
            pl.BlockSpec((seq_len, LANES), lambda b, k, s: (b, SEG_KB // LANES + k)),
            pl.BlockSpec((seq_len, LANES), lambda b, k, s: (b, SEG_VB // LANES + k)),
            pl.BlockSpec((META_PAD, LANES), lambda b, k, s: (0, SEG_KB // LANES + k)),
            pl.BlockSpec((META_PAD, LANES), lambda b, k, s: (0, SEG_VB // LANES + k)),
            pl.BlockSpec((1, META_PAD), lambda b, k, s: (0, 0)),
        ],
        out_specs=pl.BlockSpec((seq_len, qw), lambda b, k, s: (b, k)),
    )
    return pl.pallas_call(
        kernel,
        grid_spec=grid_spec,
        out_shape=jax.ShapeDtypeStruct((n, WIDTH_B), BF16),
        compiler_params=_cparams(("parallel", "parallel")),
        name="attn_b",
    )(sinks.astype(F32), proj, proj, proj, meta_proj, meta_proj, bmeta)


def _pack_halves(a):
    w = a.shape[1] // 2
    lo = pltpu.bitcast(a[:, :w].astype(BF16).astype(F32), U32)
    hi = pltpu.bitcast(a[:, w:].astype(BF16).astype(F32), U32)
    return (lo >> 16) | (hi & jnp.uint32(0xFFFF0000))


def _unpack_halves(u):
    lo = pltpu.bitcast(u << 16, F32)
    hi = pltpu.bitcast(u & jnp.uint32(0xFFFF0000), F32)
    return lo, hi


def _post_kernel(oa_ref, ob_ref, x_ref, ga_ref, gb_ref, wo_ref, gm_ref, wrh_ref, wrl_ref, br_ref,
                 tri_ref, cnt0_ref, x1_ref, xn_ref, ei_ref, gate_ref, cnt_ref, run_ref):
    i = pl.program_id(0)

    @pl.when(i == 0)
    def _():
        run_ref[...] = cnt0_ref[...]

    def normed(ref, g_ref):
        a = ref[...].astype(F32)
        ms = jnp.mean(a * a, axis=-1, keepdims=True)
        return (a * lax.rsqrt(ms + NORM_EPS) * g_ref[...]).astype(BF16)

    wa = oa_ref.shape[1]
    mix = (jnp.dot(normed(oa_ref, ga_ref), wo_ref[:wa, :], preferred_element_type=F32)
           + jnp.dot(normed(ob_ref, gb_ref), wo_ref[wa:, :], preferred_element_type=F32))
    x1 = x_ref[...] + mix
    x1_ref[...] = x1
    ms = jnp.mean(x1 * x1, axis=-1, keepdims=True)
    xn = x1 * lax.rsqrt(ms + NORM_EPS) * gm_ref[...]
    xn_ref[...] = _pack_halves(xn)

    xh = xn.astype(BF16)
    xl = (xn - xh.astype(F32)).astype(BF16)
    logits = (jnp.dot(xh, wrh_ref[...], preferred_element_type=F32)
              + jnp.dot(xl, wrh_ref[...], preferred_element_type=F32)
              + jnp.dot(xh, wrl_ref[...], preferred_element_type=F32)) + br_ref[...]

    tm = logits.shape[0]
    lane = lax.broadcasted_iota(I32, (tm, LANES), 1)
    lanef = lane.astype(F32)
    work = logits
    vals, idxs = [], []
    chosen = jnp.zeros((tm, LANES), F32)
    for _ in range(TOP_K):
        mk = jnp.max(work, axis=-1, keepdims=True)
        ik = jnp.min(jnp.where(work == mk, lanef, float(LANES)), axis=-1, keepdims=True).astype(I32)
        hit = lane == ik
        work = jnp.where(hit, -jnp.inf, work)
        chosen = jnp.where(hit, 1.0, chosen)
        vals.append(mk)
        idxs.append(ik)
    ex = [jnp.exp(v - vals[0]) for v in vals]
    den = ex[0] + ex[1] + ex[2] + ex[3]

    prefix = jnp.dot(tri_ref[...], chosen.astype(BF16), preferred_element_type=F32) + run_ref[...]
    run_ref[...] = run_ref[...] + jnp.sum(chosen, axis=0, keepdims=True)
    cnt_ref[...] = run_ref[...]

    ei = jnp.zeros((tm, LANES), I32)
    gates = jnp.zeros((tm, LANES), F32)
    for k in range(TOP_K):
        rank = jnp.sum(jnp.where(lane == idxs[k], prefix, 0.0), axis=-1, keepdims=True).astype(I32)
        ei = jnp.where(lane == k, idxs[k], ei)
        ei = jnp.where(lane == TOP_K + k, rank, ei)
        gates = jnp.where(lane == k, ex[k] / den, gates)
    ei_ref[...] = ei
    gate_ref[...] = gates


def _post(oa, ob, x2d, ga, gb, wo, gm, wrh, wrl, br, cnt0):
    n, d = x2d.shape
    tm = 256
    assert n % tm == 0
    tri = jnp.asarray(np.tril(np.ones((tm, tm), np.float32), -1), BF16)
    row = lambda i: (i, 0)
    fixed = lambda i: (0, 0)
    return pl.pallas_call(
        _post_kernel,
        grid=(n // tm,),
        in_specs=[
            pl.BlockSpec((tm, oa.shape[1]), row),
            pl.BlockSpec((tm, ob.shape[1]), row),
            pl.BlockSpec((tm, d), row),
            pl.BlockSpec((1, oa.shape[1]), fixed),
            pl.BlockSpec((1, ob.shape[1]), fixed),
            pl.BlockSpec(wo.shape, fixed),
            pl.BlockSpec((1, d), fixed),
            pl.BlockSpec((d, LANES), fixed),
            pl.BlockSpec((d, LANES), fixed),
            pl.BlockSpec((1, LANES), fixed),
            pl.BlockSpec((tm, tm), fixed),
            pl.BlockSpec((1, LANES), fixed),
        ],
        out_specs=[
            pl.BlockSpec((tm, d), row),
            pl.BlockSpec((tm, d // 2), row),
            pl.BlockSpec((tm, LANES), row),
            pl.BlockSpec((tm, LANES), row),
            pl.BlockSpec((1, LANES), fixed),
        ],
        out_shape=[
            jax.ShapeDtypeStruct((n, d), F32),
            jax.ShapeDtypeStruct((n, d // 2), U32),
            jax.ShapeDtypeStruct((n, LANES), I32),
            jax.ShapeDtypeStruct((n, LANES), F32),
            jax.ShapeDtypeStruct((1, LANES), F32),
        ],
        scratch_shapes=[pltpu.VMEM((1, LANES), F32)],
        compiler_params=_cparams(("arbitrary",)),
        name="post",
    )(oa, ob, x2d, ga, gb, wo, gm, wrh, wrl, br, tri, cnt0)


def _dispatch_kernel(slot_ref, xn_ref, xs_in_ref, xs_ref, sems, *, tk):
    del xs_in_ref

    def row_copy(t, k):
        dst = slot_ref[0, 0, t * TOP_K + k]
        return pltpu.make_async_copy(xn_ref.at[pl.ds(t, 1)], xs_ref.at[pl.ds(dst, 1)], sems.at[k])

    def issue(t, carry):
        for k in range(TOP_K):
            row_copy(t, k).start()
        return carry

    lax.fori_loop(0, tk, issue, 0)
    for k in range(TOP_K):
        pltpu.make_async_copy(xn_ref, xs_ref.at[pl.ds(0, tk)], sems.at[k]).wait()


def _dispatch(slots, xn, xs):
    n, w = xn.shape
    tk = 512
    assert n % tk == 0
    slots3 = slots.reshape(n // tk, 1, tk * TOP_K)
    kernel = functools.partial(_dispatch_kernel, tk=tk)
    return pl.pallas_call(
        kernel,
        grid=(n // tk,),
        in_specs=[
            pl.BlockSpec((1, 1, tk * TOP_K), lambda i: (i, 0, 0), memory_space=pltpu.SMEM),
            pl.BlockSpec((tk, w), lambda i: (i, 0)),
            pl.BlockSpec(memory_space=pl.ANY),
        ],
        out_specs=pl.BlockSpec(memory_space=pl.ANY),
        out_shape=jax.ShapeDtypeStruct(xs.shape, xs.dtype),
        input_output_aliases={2: 0},
        scratch_shapes=[pltpu.SemaphoreType.DMA((TOP_K,))],
        compiler_params=_cparams(("arbitrary",)),
        name="dispatch",
    )(slots3, xn, xs)


def _experts_kernel(bexp_ref, bval_ref, xs_ref, wg_ref, bg_ref, wu_ref, bu_ref, wd_ref, bd_ref,
                    ys_ref, lo_ref, hi_ref, acc_ref):
    del bexp_ref
    i = pl.program_id(0)
    j = pl.program_id(1)
    nvalid = bval_ref[i]

    @pl.when(jnp.logical_and(nvalid == 0, j == 0))
    def _():
        ys_ref[...] = jnp.zeros_like(ys_ref)

    @pl.when(nvalid > 0)
    def _():
        @pl.when(j == 0)
        def _():
            u = xs_ref[...]
            rows = lax.broadcasted_iota(I32, u.shape, 0)
            u = jnp.where(rows < nvalid, u, jnp.zeros_like(u))
            lo, hi = _unpack_halves(u)
            lo_ref[...] = lo.astype(BF16)
            hi_ref[...] = hi.astype(BF16)
            acc_ref[...] = jnp.zeros_like(acc_ref)

        half = lo_ref.shape[1]
        lo = lo_ref[...]
        hi = hi_ref[...]
        hg = (jnp.dot(lo, wg_ref[0, :half, :], preferred_element_type=F32)
              + jnp.dot(hi, wg_ref[0, half:, :], preferred_element_type=F32) + bg_ref[0])
        hu = (jnp.dot(lo, wu_ref[0, :half, :], preferred_element_type=F32)
              + jnp.dot(hi, wu_ref[0, half:, :], preferred_element_type=F32) + bu_ref[0])
        g = jnp.minimum(hg, SWIGLU_LIMIT)
        u = jnp.clip(hu, -SWIGLU_LIMIT, SWIGLU_LIMIT)
        act = (u + 1.0) * (g * jax.nn.sigmoid(SWIGLU_ALPHA * g))
        acc_ref[...] += jnp.dot(act.astype(BF16), wd_ref[0], preferred_element_type=F32)

        @pl.when(j == pl.num_programs(1) - 1)
        def _():
            ys_ref[...] = _pack_halves(acc_ref[...] + bd_ref[0])


def _experts(blk_exp, blk_valid, xs, wg, bg, wu, bu, wd, bd, tm, tf):
    p, half = xs.shape
    d = 2 * half
    dff = wg.shape[2]
    assert p % tm == 0 and dff % tf == 0
    nblk = p // tm
    nf = dff // tf

    def jeff(i, j, bval):
        return jnp.where(bval[i] > 0, j, nf - 1)

    grid_spec = pltpu.PrefetchScalarGridSpec(
        num_scalar_prefetch=2,
        grid=(nblk, nf),
        in_specs=[
            pl.BlockSpec((tm, half), lambda i, j, be, bv: (i, 0)),
            pl.BlockSpec((1, d, tf), lambda i, j, be, bv: (be[i], 0, jeff(i, j, bv))),
            pl.BlockSpec((1, 1, tf), lambda i, j, be, bv: (be[i], 0, jeff(i, j, bv))),
            pl.BlockSpec((1, d, tf), lambda i, j, be, bv: (be[i], 0, jeff(i, j, bv))),
            pl.BlockSpec((1, 1, tf), lambda i, j, be, bv: (be[i], 0, jeff(i, j, bv))),
            pl.BlockSpec((1, tf, d), lambda i, j, be, bv: (be[i], jeff(i, j, bv), 0)),
            pl.BlockSpec((1, 1, d), lambda i, j, be, bv: (be[i], 0, 0)),
        ],
        out_specs=pl.BlockSpec((tm, half), lambda i, j, be, bv: (i, 0)),
        scratch_shapes=[pltpu.VMEM((tm, half), BF16), pltpu.VMEM((tm, half), BF16),
                        pltpu.VMEM((tm, d), F32)],
    )
    return pl.pallas_call(
        _experts_kernel,
        grid_spec=grid_spec,
        out_shape=jax.ShapeDtypeStruct((p, half), U32),
        compiler_params=_cparams(("arbitrary", "arbitrary")),
        name="experts",
    )(blk_exp, blk_valid, xs, wg, bg, wu, bu, wd, bd)


def _combine_kernel(slot_ref, slot_next_ref, x1_ref, gate_ref, ys_ref, o_ref, buf_ref, sems, *, tk):
    i = pl.program_id(0)
    cur = i % 2

    def gather(slots, b):
        def issue(t, carry):
            for k in range(TOP_K):
                src = slots[0, 0, t * TOP_K + k]
                pltpu.make_async_copy(ys_ref.at[pl.ds(src, 1)], buf_ref.at[b, k, pl.ds(t, 1)],
                                      sems.at[b, k]).start()
            return carry
        lax.fori_loop(0, tk, issue, 0)

    @pl.when(i == 0)
    def _():
        gather(slot_ref, 0)

    @pl.when(i + 1 < pl.num_programs(0))
    def _():
        gather(slot_next_ref, 1 - cur)

    half = buf_ref.shape[3]
    gates = gate_ref[...]
    acc_lo = x1_ref[:, :half]
    acc_hi = x1_ref[:, half:]
    for k in range(TOP_K):
        pltpu.make_async_copy(ys_ref.at[pl.ds(0, tk)], buf_ref.at[cur, k], sems.at[cur, k]).wait()
        lo, hi = _unpack_halves(buf_ref[cur, k])
        g = gates[:, k:k + 1]
        acc_lo = acc_lo + g * lo
        acc_hi = acc_hi + g * hi
    o_ref[:, :half] = acc_lo
    o_ref[:, half:] = acc_hi


def _combine(slots, x1, gates, ys):
    n, d = x1.shape
    tk = 256
    assert n % tk == 0
    steps = n // tk
    slots3 = slots.reshape(steps, 1, tk * TOP_K)
    kernel = functools.partial(_combine_kernel, tk=tk)
    return pl.pallas_call(
        kernel,
        grid=(steps,),
        in_specs=[
            pl.BlockSpec((1, 1, tk * TOP_K), lambda i: (i, 0, 0), memory_space=pltpu.SMEM),
            pl.BlockSpec((1, 1, tk * TOP_K), lambda i: (jnp.minimum(i + 1, steps - 1), 0, 0),
                         memory_space=pltpu.SMEM),
            pl.BlockSpec((tk, d), lambda i: (i, 0)),
            pl.BlockSpec((tk, LANES), lambda i: (i, 0)),
            pl.BlockSpec(memory_space=pl.ANY),
        ],
        out_specs=pl.BlockSpec((tk, d), lambda i: (i, 0)),
        out_shape=jax.ShapeDtypeStruct((n, d), F32),
        scratch_shapes=[pltpu.VMEM((2, TOP_K, tk, d // 2), U32), pltpu.SemaphoreType.DMA((2, TOP_K))],
        compiler_params=_cparams(("arbitrary",)),
        name="combine",
    )(slots3, slots3, x1, gates, ys)


MOE_TM = 512
MOE_TF = 1024


def _dup_kv_heads(w):
    d = w.shape[0]
    w = w.reshape(d, N_KV_B, 1, HEAD_DIM)
    return jnp.broadcast_to(w, (d, N_KV_B, 2, HEAD_DIM)).reshape(d, 2 * KV_WIDTH_B)


def kernel(x_prompt, x_sample, meta_tokens, norm_attn, w_in, q_norm_a, k_norm_a, rpb, rpb_meta, q_norm_b, k_norm_b, sinks, out_norm_a, out_norm_b, w_out, norm_mlp, w_router, b_router, w_gate, b_gate, w_up, b_up, w_down, b_down):
    depth = norm_attn.shape[0]
    assert depth == 1
    d = x_prompt.shape[-1]
    groups = [(x_prompt.reshape(-1, d), x_prompt.shape[0], x_prompt.shape[1]),
              (x_sample.reshape(-1, d), x_sample.shape[0], x_sample.shape[1])]

    wi = w_in[0]
    w_cat = jnp.concatenate([wi[:, :SEG_KB], _dup_kv_heads(wi[:, SEG_KB:SEG_KB + KV_WIDTH_B]),
                             _dup_kv_heads(wi[:, SEG_KB + KV_WIDTH_B:])], axis=1).astype(BF16)
    scale = HEAD_DIM ** -0.5
    gcol = jnp.concatenate([
        jnp.tile(q_norm_a[0].astype(F32) * scale, N_HEADS_A), jnp.tile(k_norm_a[0].astype(F32), N_HEADS_A),
        jnp.ones((WIDTH_A,), F32),
        jnp.tile(q_norm_b[0].astype(F32) * scale, N_HEADS_B), jnp.tile(k_norm_b[0].astype(F32), 2 * N_KV_B),
        jnp.ones((2 * KV_WIDTH_B,), F32)])[None, :]
    bd = jnp.asarray(np.kron(np.eye(LANES // HEAD_DIM, dtype=np.float32),
                             np.ones((HEAD_DIM, HEAD_DIM), np.float32)), BF16)
    g_attn = norm_attn[0].astype(F32)[None, :]
    bias_tab, bias_meta = _bias_tables_a(rpb[0], rpb_meta[0])
    wo = w_out[0].astype(BF16)
    ga = out_norm_a[0].astype(F32)[None, :]
    gb = out_norm_b[0].astype(F32)[None, :]
    gm = norm_mlp[0].astype(F32)[None, :]
    wr = jnp.zeros((d, LANES), F32).at[:, :N_EXPERTS].set(w_router[0].astype(F32))
    wrh = wr.astype(BF16)
    wrl = (wr - wrh.astype(F32)).astype(BF16)
    br = jnp.full((1, LANES), NEG_INF, F32).at[0, :N_EXPERTS].set(b_router[0].astype(F32))
    wg = w_gate[0].astype(BF16)
    wu = w_up[0].astype(BF16)
    wd = w_down[0].astype(BF16)
    bg = b_gate[0].astype(F32)[:, None, :]
    bu = b_up[0].astype(F32)[:, None, :]
    bdn = b_down[0].astype(F32)[:, None, :]

    meta_x = jnp.zeros((META_PAD, d), F32).at[:N_META].set(meta_tokens.astype(F32))
    meta_pos = np.minimum(np.arange(META_PAD), N_META - 1)
    meta_proj = _proj(meta_x, META_PAD, meta_pos, g_attn, w_cat, gcol, bd)
    meta_proj = jnp.where(jnp.arange(META_PAD)[:, None] < N_META, meta_proj, jnp.zeros_like(meta_proj))

    cnt = jnp.zeros((1, LANES), F32)
    staged = []
    for x2d, batch, seq_len in groups:
        pos = N_META + np.arange(seq_len)
        proj = _proj(x2d, seq_len, pos, g_attn, w_cat, gcol, bd)
        oa = _attn_a(proj, meta_proj, bias_tab, bias_meta, batch, seq_len)
        ob = _attn_b(proj, meta_proj, sinks[0], batch, seq_len)
        x1, xn, ei, gates, cnt = _post(oa, ob, x2d, ga, gb, wo, gm, wrh, wrl, br, cnt)
        staged.append((x1, xn, ei, gates))

    total = sum(g[0].shape[0] for g in groups) * TOP_K
    nblk = (total + N_EXPERTS * (MOE_TM - 1) + MOE_TM - 1) // MOE_TM
    counts = cnt[0, :N_EXPERTS].astype(I32)
    padded = (counts + MOE_TM - 1) // MOE_TM * MOE_TM
    pends = jnp.cumsum(padded)
    pstarts = pends - padded
    blk_start = jnp.arange(nblk, dtype=I32) * MOE_TM
    blk_exp = jnp.minimum(jnp.sum((blk_start[:, None] >= pends[None, :]).astype(I32), axis=1), N_EXPERTS - 1)
    blk_valid = jnp.clip(counts[blk_exp] - (blk_start - pstarts[blk_exp]), 0, MOE_TM)
    blk_valid = jnp.where(blk_start < pends[-1], blk_valid, 0).astype(I32)
    last_used = jnp.maximum(pends[-1] // MOE_TM - 1, 0)
    blk_exp = jnp.where(blk_start < pends[-1], blk_exp, blk_exp[last_used]).astype(I32)

    slots = [pstarts[ei[:, :TOP_K]] + ei[:, TOP_K:2 * TOP_K] for (_, _, ei, _) in staged]

    xs = jnp.zeros((nblk * MOE_TM, d // 2), U32)
    for (x1, xn, ei, gates), sl in zip(staged, slots):
        xs = _dispatch(sl, xn, xs)
    ys = _experts(blk_exp, blk_valid, xs, wg, bg, wu, bu, wd, bdn, MOE_TM, MOE_TF)

    outs = []
    for (x1, xn, ei, gates), sl, (x2d, batch, seq_len) in zip(staged, slots, groups):
        y = _combine(sl, x1, gates, ys)
        outs.append(y.reshape(batch, seq_len, d))
    return tuple(outs)
```

```python
import functools

import jax
import jax.numpy as jnp
import numpy as np
from jax import lax
from jax.experimental import pallas as pl
from jax.experimental.pallas import tpu as pltpu

F32 = jnp.float32
BF16 = jnp.bfloat16
U32 = jnp.uint32
I32 = jnp.int32

HEAD_DIM = 64
N_HEADS_A = 16
N_HEADS_B = 16
N_KV_B = 4
WIDTH_A = N_HEADS_A * HEAD_DIM
WIDTH_B = N_HEADS_B * HEAD_DIM
KV_WIDTH_B = N_KV_B * HEAD_DIM
GRID_W = 64
NA_KH = 8
NA_KW = 16
WINDOW = 128
WBLOCK = 128
ROT_DIM = HEAD_DIM // 4
ROPE_THETA = 500000.0
N_META = 16
N_EXPERTS = 32
TOP_K = 4
SWIGLU_LIMIT = 7.0
SWIGLU_ALPHA = 1.702
NORM_EPS = 1e-5
NEG_INF = -1e30

LANES = 128
META_PAD = LANES
VMEM_LIMIT = 56 * 1024 * 1024

PROJ_TN = 512
SEG_QA, SEG_KA, SEG_VA, SEG_QB = 0, WIDTH_A, 2 * WIDTH_A, 3 * WIDTH_A
SEG_KB = 3 * WIDTH_A + WIDTH_B
SEG_VB = SEG_KB + 2 * KV_WIDTH_B
PROJ_COLS = SEG_VB + 2 * KV_WIDTH_B


def _cparams(sem):
    return pltpu.CompilerParams(dimension_semantics=sem, vmem_limit_bytes=VMEM_LIMIT)


def _proj_kernel(x_ref, g_ref, w_ref, gcol_ref, bd_ref, c_ref, s1_ref, s2_ref, o_ref, h_ref):
    j = pl.program_id(1)

    @pl.when(j == 0)
    def _():
        x = x_ref[...]
        ms = jnp.mean(x * x, axis=-1, keepdims=True)
        h_ref[...] = (x * lax.rsqrt(ms + NORM_EPS) * g_ref[...]).astype(BF16)

    acc = jnp.dot(h_ref[...], w_ref[...], preferred_element_type=F32)
    is_v = jnp.logical_or(j == SEG_VA // PROJ_TN, j == SEG_VA // PROJ_TN + 1)
    is_v = jnp.logical_or(is_v, j == SEG_VB // PROJ_TN)
    is_rope = jnp.logical_and(j >= SEG_QB // PROJ_TN, j < SEG_VB // PROJ_TN)

    @pl.when(is_v)
    def _():
        o_ref[...] = acc.astype(o_ref.dtype)

    def head_normed(c):
        a = acc[:, c * LANES:(c + 1) * LANES]
        ssq = jnp.dot((a * a).astype(BF16), bd_ref[...], preferred_element_type=F32)
        return a * lax.rsqrt(ssq * (1.0 / HEAD_DIM) + NORM_EPS) * gcol_ref[:, c * LANES:(c + 1) * LANES]

    @pl.when(jnp.logical_and(jnp.logical_not(is_v), jnp.logical_not(is_rope)))
    def _():
        for c in range(PROJ_TN // LANES):
            o_ref[:, c * LANES:(c + 1) * LANES] = head_normed(c).astype(o_ref.dtype)

    @pl.when(is_rope)
    def _():
        for c in range(PROJ_TN // LANES):
            y = head_normed(c)
            up = pltpu.roll(y, LANES - ROT_DIM // 2, 1)
            dn = pltpu.roll(y, ROT_DIM // 2, 1)
            r = y * c_ref[...] + up * s1_ref[...] + dn * s2_ref[...]
            o_ref[:, c * LANES:(c + 1) * LANES] = r.astype(o_ref.dtype)


def _rope_tables(positions):
    inv = ROPE_THETA ** (-np.arange(0, ROT_DIM, 2, dtype=np.float32) / ROT_DIM)
    ang = positions.astype(np.float32)[:, None] * inv[None, :]
    cos, sin = np.cos(ang), np.sin(ang)
    half = ROT_DIM // 2
    n = positions.shape[0]
    c = np.ones((n, HEAD_DIM), np.float32)
    s1 = np.zeros((n, HEAD_DIM), np.float32)
    s2 = np.zeros((n, HEAD_DIM), np.float32)
    c[:, :half] = cos
    c[:, half:ROT_DIM] = cos
    s1[:, :half] = -sin
    s2[:, half:ROT_DIM] = sin
    rep = LANES // HEAD_DIM
    return tuple(jnp.asarray(np.tile(t, (1, rep))) for t in (c, s1, s2))


def _proj(x2d, seq_len, positions, g_attn, w_cat, gcol, bd):
    n, d = x2d.shape
    tm = min(1024, seq_len)
    assert seq_len % tm == 0 and n % tm == 0
    per_seq = seq_len // tm
    c, s1, s2 = _rope_tables(positions)
    grid = (n // tm, PROJ_COLS // PROJ_TN)
    tab = pl.BlockSpec((tm, LANES), lambda i, j: (i % per_seq, 0))
    return pl.pallas_call(
        _proj_kernel,
        grid=grid,
        in_specs=[
            pl.BlockSpec((tm, d), lambda i, j: (i, 0)),
            pl.BlockSpec((1, d), lambda i, j: (0, 0)),
            pl.BlockSpec((d, PROJ_TN), lambda i, j: (0, j)),
            pl.BlockSpec((1, PROJ_TN), lambda i, j: (0, j)),
            pl.BlockSpec((LANES, LANES), lambda i, j: (0, 0)),
            tab, tab, tab,
        ],
        out_specs=pl.BlockSpec((tm, PROJ_TN), lambda i, j: (i, j)),
        out_shape=jax.ShapeDtypeStruct((n, PROJ_COLS), BF16),
        scratch_shapes=[pltpu.VMEM((tm, d), BF16)],
        compiler_params=_cparams(("parallel", "arbitrary")),
        name="proj",
    )(x2d, g_attn, w_cat, gcol, bd, c, s1, s2)


ATTN_A_UNROLL = 8


def _attn_a_kernel(q_ref, k_ref, v_ref, km_ref, vm_ref, bias_ref, bmeta_ref, o_ref, *, rows):
    lane = lax.broadcasted_iota(I32, (GRID_W, LANES), 1)
    first = lane < HEAD_DIM
    km = km_ref[...]
    vm = vm_ref[...]
    bmeta = bmeta_ref[0]
    nt = (((1,), (1,)), ((), ()))

    def row_body(r, carry):
        r0 = jnp.clip(r - NA_KH // 2, 0, rows - NA_KH)
        var = r - r0
        q = q_ref[pl.ds(pl.multiple_of(r * GRID_W, GRID_W), GRID_W), :]
        zero = jnp.zeros_like(q)
        qs = jnp.concatenate([jnp.where(first, q, zero), jnp.where(first, zero, q)], axis=0)
        ks = pl.multiple_of(r0 * GRID_W, GRID_W)
        kb = k_ref[pl.ds(ks, NA_KH * GRID_W), :]
        vb = v_ref[pl.ds(ks, NA_KH * GRID_W), :]
        s = lax.dot_general(qs, kb, nt, preferred_element_type=F32) + bias_ref[0, var]
        sm = lax.dot_general(qs, km, nt, preferred_element_type=F32) + bmeta
        m = jnp.maximum(jnp.max(s, axis=-1, keepdims=True), jnp.max(sm, axis=-1, keepdims=True))
        p = jnp.exp(s - m)
        pm = jnp.exp(sm - m)
        l = jnp.sum(p, axis=-1, keepdims=True) + jnp.sum(pm, axis=-1, keepdims=True)
        o = (jnp.dot(p.astype(BF16), vb, preferred_element_type=F32)
             + jnp.dot(pm.astype(BF16), vm, preferred_element_type=F32)) / l
        o_ref[pl.ds(pl.multiple_of(r * GRID_W, GRID_W), GRID_W), :] = (
            jnp.where(first, o[:GRID_W], o[GRID_W:]).astype(o_ref.dtype))
        return carry

    lax.fori_loop(0, rows, row_body, 0, unroll=ATTN_A_UNROLL)


def _attn_a(proj, meta_proj, bias_tab, bias_meta, batch, seq_len):
    rows = seq_len // GRID_W
    assert rows >= NA_KH and rows % ATTN_A_UNROLL == 0
    n = batch * seq_len
    pairs = WIDTH_A // LANES
    kernel = functools.partial(_attn_a_kernel, rows=rows)
    return pl.pallas_call(
        kernel,
        grid=(batch, pairs),
        in_specs=[
            pl.BlockSpec((seq_len, LANES), lambda b, p: (b, SEG_QA // LANES + p)),
            pl.BlockSpec((seq_len, LANES), lambda b, p: (b, SEG_KA // LANES + p)),
            pl.BlockSpec((seq_len, LANES), lambda b, p: (b, SEG_VA // LANES + p)),
            pl.BlockSpec((META_PAD, LANES), lambda b, p: (0, SEG_KA // LANES + p)),
            pl.BlockSpec((META_PAD, LANES), lambda b, p: (0, SEG_VA // LANES + p)),
            pl.BlockSpec((1, NA_KH, 2 * GRID_W, NA_KH * GRID_W), lambda b, p: (p, 0, 0, 0)),
            pl.BlockSpec((1, 2 * GRID_W, META_PAD), lambda b, p: (p, 0, 0)),
        ],
        out_specs=pl.BlockSpec((seq_len, LANES), lambda b, p: (b, p)),
        out_shape=jax.ShapeDtypeStruct((n, WIDTH_A), BF16),
        compiler_params=_cparams(("parallel", "parallel")),
        name="attn_a",
    )(proj, proj, proj, meta_proj, meta_proj, bias_tab, bias_meta)


def _bias_tables_a(rpb, rpb_meta):
    n_dr, n_dc = 2 * NA_KH - 1, 2 * NA_KW - 1
    var = np.arange(NA_KH)[:, None]
    jj = np.arange(NA_KH)[None, :]
    ridx = jj - var + NA_KH - 1
    cq = np.arange(GRID_W)
    col_start = np.clip(cq - NA_KW // 2, 0, GRID_W - NA_KW)
    col_valid = (cq[None, :] >= col_start[:, None]) & (cq[None, :] < col_start[:, None] + NA_KW)
    cidx = np.clip(cq[None, :] - cq[:, None], -(NA_KW - 1), NA_KW - 1) + NA_KW - 1
    sel_r = (ridx[None, :, :] == np.arange(n_dr)[:, None, None]).astype(np.float32)
    sel_c = (cidx[None, :, :] == np.arange(n_dc)[:, None, None]).astype(np.float32)
    hi = lax.Precision.HIGHEST
    t1 = jnp.einsum('hab,bcw->hacw', rpb.astype(F32), jnp.asarray(sel_c), precision=hi)
    tab = jnp.einsum('avj,hacw->hvcjw', jnp.asarray(sel_r), t1, precision=hi)
    tab = jnp.where(col_valid[None, None, :, None, :], tab, NEG_INF)
    tab = tab.reshape(N_HEADS_A // 2, 2, NA_KH, GRID_W, NA_KH * GRID_W)
    tab = jnp.transpose(tab, (0, 2, 1, 3, 4)).reshape(N_HEADS_A // 2, NA_KH, 2 * GRID_W, NA_KH * GRID_W)
    bm = jnp.full((N_HEADS_A, META_PAD), NEG_INF, F32).at[:, :N_META].set(rpb_meta.astype(F32))
    bm = jnp.broadcast_to(bm[:, None, :], (N_HEADS_A, GRID_W, META_PAD)).reshape(N_HEADS_A // 2, 2 * GRID_W, META_PAD)
    return tab, bm


ATTN_B_UNROLL = 4


def _attn_b_kernel(sink_ref, q_ref, k_ref, v_ref, km_ref, vm_ref, bmeta_ref, o_ref, *, seq_len):
    kv = pl.program_id(1)
    nb = seq_len // WBLOCK
    span = 3 * WBLOCK
    group = N_HEADS_B // N_KV_B
    stack = group * WBLOCK
    lane = lax.broadcasted_iota(I32, (WBLOCK, LANES), 1)
    first = lane < HEAD_DIM
    km = km_ref[...]
    vm = vm_ref[...]
    bmeta = bmeta_ref[...]
    nt = (((1,), (1,)), ((), ()))
    qi = lax.broadcasted_iota(I32, (stack, span), 0) % WBLOCK
    kj = lax.broadcasted_iota(I32, (stack, span), 1)
    rel = kj - qi
    head_of_row = lax.broadcasted_iota(I32, (stack, 1), 0) // WBLOCK
    sink = jnp.zeros((stack, 1), F32)
    for g in range(group):
        sink = jnp.where(head_of_row == g, sink_ref[kv * group + g], sink)

    def blk_body(n, carry):
        start = jnp.clip((n - 1) * WBLOCK, 0, seq_len - span)
        start = pl.multiple_of(start, WBLOCK)
        q0 = pl.multiple_of(n * WBLOCK, WBLOCK)
        kb = k_ref[pl.ds(start, span), :]
        vb = v_ref[pl.ds(start, span), :]
        parts = []
        for c in range(group // 2):
            q = q_ref[pl.ds(q0, WBLOCK), c * LANES:(c + 1) * LANES]
            zero = jnp.zeros_like(q)
            parts += [jnp.where(first, q, zero), jnp.where(first, zero, q)]
        qs = jnp.concatenate(parts, axis=0)
        s = lax.dot_general(qs, kb, nt, preferred_element_type=F32)
        s = jnp.where(jnp.abs(rel + (start - q0)) <= WINDOW, s, NEG_INF)
        sm = lax.dot_general(qs, km, nt, preferred_element_type=F32) + bmeta
        m = jnp.maximum(jnp.max(s, axis=-1, keepdims=True), jnp.max(sm, axis=-1, keepdims=True))
        m = jnp.maximum(m, sink)
        p = jnp.exp(s - m)
        pm = jnp.exp(sm - m)
        l = jnp.sum(p, axis=-1, keepdims=True) + jnp.sum(pm, axis=-1, keepdims=True) + jnp.exp(sink - m)
        o = (jnp.dot(p.astype(BF16), vb, preferred_element_type=F32)
             + jnp.dot(pm.astype(BF16), vm, preferred_element_type=F32)) / l
        for c in range(group // 2):
            base = 2 * c * WBLOCK
            o_ref[pl.ds(q0, WBLOCK), c * LANES:(c + 1) * LANES] = jnp.where(
                first, o[base:base + WBLOCK], o[base + WBLOCK:base + 2 * WBLOCK]).astype(o_ref.dtype)
        return carry

    lax.fori_loop(0, nb, blk_body, 0, unroll=ATTN_B_UNROLL)


def _attn_b(proj, meta_proj, sinks, batch, seq_len):
    assert seq_len % (WBLOCK * ATTN_B_UNROLL) == 0 and seq_len >= 3 * WBLOCK
    n = batch * seq_len
    qw = WIDTH_B // N_KV_B
    bmeta = jnp.where(jnp.arange(META_PAD) < N_META, 0.0, NEG_INF).astype(F32)[None, :]
    kernel = functools.partial(_attn_b_kernel, seq_len=seq_len)
    grid_spec = pltpu.PrefetchScalarGridSpec(
        num_scalar_prefetch=1,
        grid=(batch, N_KV_B),
        in_specs=[
            pl.BlockSpec((seq_len, qw), lambda b, k, s: (b, SEG_QB // qw + k)),
            pl.BlockSpec((seq_len, LANES), lambda b, k, s: (b, SEG_KB // LANES + k)),
            pl.BlockSpec((seq_len, LANES), lambda b, k, s: (b, SEG_VB // LANES + k)),
            pl.BlockSpec((META_PAD, LANES), lambda b, k, s: (0, SEG_KB // LANES + k)),
            pl.BlockSpec((META_PAD, LANES), lambda b, k, s: (0, SEG_VB // LANES + k)),
            pl.BlockSpec((1, META_PAD), lambda b, k, s: (0, 0)),
        ],
        out_specs=pl.BlockSpec((seq_len, qw), lambda b, k, s: (b, k)),
    )
    return pl.pallas_call(
        kernel,
        grid_spec=grid_spec,
        out_shape=jax.ShapeDtypeStruct((n, WIDTH_B), BF16),
        compiler_params=_cparams(("parallel", "parallel")),
        name="attn_b",
    )(sinks.astype(F32), proj, proj, proj, meta_proj, meta_proj, bmeta)


def _pack_halves(a):
    w = a.shape[1] // 2
    lo = pltpu.bitcast(a[:, :w].astype(BF16).astype(F32), U32)
    hi = pltpu.bitcast(a[:, w:].astype(BF16).astype(F32), U32)
    return (lo >> 16) | (hi & jnp.uint32(0xFFFF0000))


def _unpack_halves(u):
    lo = pltpu.bitcast(u << 16, F32)
    hi = pltpu.bitcast(u & jnp.uint32(0xFFFF0000), F32)
    return lo, hi


def _post_kernel(oa_ref, ob_ref, x_ref, ga_ref, gb_ref, wo_ref, gm_ref, wrh_ref, wrl_ref, br_ref,
                 tri_ref, cnt0_ref, x1_ref, xn_ref, ei_ref, gate_ref, cnt_ref, run_ref):
    i = pl.program_id(0)

    @pl.when(i == 0)
    def _():
        run_ref[...] = cnt0_ref[...]

    def normed(ref, g_ref):
        a = ref[...].astype(F32)
        ms = jnp.mean(a * a, axis=-1, keepdims=True)
        return (a * lax.rsqrt(ms + NORM_EPS) * g_ref[...]).astype(BF16)

    wa = oa_ref.shape[1]
    mix = (jnp.dot(normed(oa_ref, ga_ref), wo_ref[:wa, :], preferred_element_type=F32)
           + jnp.dot(normed(ob_ref, gb_ref), wo_ref[wa:, :], preferred_element_type=F32))
    x1 = x_ref[...] + mix
    x1_ref[...] = x1
    ms = jnp.mean(x1 * x1, axis=-1, keepdims=True)
    xn = x1 * lax.rsqrt(ms + NORM_EPS) * gm_ref[...]
    xn_ref[...] = _pack_halves(xn)

    xh = xn.astype(BF16)
    xl = (xn - xh.astype(F32)).astype(BF16)
    logits = (jnp.dot(xh, wrh_ref[...], preferred_element_type=F32)
              + jnp.dot(xl, wrh_ref[...], preferred_element_type=F32)
              + jnp.dot(xh, wrl_ref[...], preferred_element_type=F32)) + br_ref[...]

    tm = logits.shape[0]
    lane = lax.broadcasted_iota(I32, (tm, LANES), 1)
    lanef = lane.astype(F32)
    work = logits
    vals, idxs = [], []
    chosen = jnp.zeros((tm, LANES), F32)
    for _ in range(TOP_K):
        mk = jnp.max(work, axis=-1, keepdims=True)
        ik = jnp.min(jnp.where(work == mk, lanef, float(LANES)), axis=-1, keepdims=True).astype(I32)
        hit = lane == ik
        work = jnp.where(hit, -jnp.inf, work)
        chosen = jnp.where(hit, 1.0, chosen)
        vals.append(mk)
        idxs.append(ik)
    ex = [jnp.exp(v - vals[0]) for v in vals]
    den = ex[0] + ex[1] + ex[2] + ex[3]

    prefix = jnp.dot(tri_ref[...], chosen.astype(BF16), preferred_element_type=F32) + run_ref[...]
    run_ref[...] = run_ref[...] + jnp.sum(chosen, axis=0, keepdims=True)
    cnt_ref[...] = run_ref[...]

    ei = jnp.zeros((tm, LANES), I32)
    gates = jnp.zeros((tm, LANES), F32)
    for k in range(TOP_K):
        rank = jnp.sum(jnp.where(lane == idxs[k], prefix, 0.0), axis=-1, keepdims=True).astype(I32)
        ei = jnp.where(lane == k, idxs[k], ei)
        ei = jnp.where(lane == TOP_K + k, rank, ei)
        gates = jnp.where(lane == k, ex[k] / den, gates)
    ei_ref[...] = ei
    gate_ref[...] = gates


def _post(oa, ob, x2d, ga, gb, wo, gm, wrh, wrl, br, cnt0):
    n, d = x2d.shape
    tm = 256
    assert n % tm == 0
    tri = jnp.asarray(np.tril(np.ones((tm, tm), np.float32), -1), BF16)
    row = lambda i: (i, 0)
    fixed = lambda i: (0, 0)
    return pl.pallas_call(
        _post_kernel,
        grid=(n // tm,),
        in_specs=[
            pl.BlockSpec((tm, oa.shape[1]), row),
            pl.BlockSpec((tm, ob.shape[1]), row),
            pl.BlockSpec((tm, d), row),
            pl.BlockSpec((1, oa.shape[1]), fixed),
            pl.BlockSpec((1, ob.shape[1]), fixed),
            pl.BlockSpec(wo.shape, fixed),
            pl.BlockSpec((1, d), fixed),
            pl.BlockSpec((d, LANES), fixed),
            pl.BlockSpec((d, LANES), fixed),
            pl.BlockSpec((1, LANES), fixed),
            pl.BlockSpec((tm, tm), fixed),
            pl.BlockSpec((1, LANES), fixed),
        ],
        out_specs=[
            pl.BlockSpec((tm, d), row),
            pl.BlockSpec((tm, d // 2), row),
            pl.BlockSpec((tm, LANES), row),
            pl.BlockSpec((tm, LANES), row),
            pl.BlockSpec((1, LANES), fixed),
        ],
        out_shape=[
            jax.ShapeDtypeStruct((n, d), F32),
            jax.ShapeDtypeStruct((n, d // 2), U32),
            jax.ShapeDtypeStruct((n, LANES), I32),
            jax.ShapeDtypeStruct((n, LANES), F32),
            jax.ShapeDtypeStruct((1, LANES), F32),
        ],
        scratch_shapes=[pltpu.VMEM((1, LANES), F32)],
        compiler_params=_cparams(("arbitrary",)),
        name="post",
    )(oa, ob, x2d, ga, gb, wo, gm, wrh, wrl, br, tri, cnt0)


ROW_DMA_UNROLL = 4


def _dispatch_kernel(slot_ref, xn_ref, xs_in_ref, xs_ref, sems, *, tk):
    del xs_in_ref

    def row_copy(t, k):
        dst = slot_ref[0, 0, t * TOP_K + k]
        return pltpu.make_async_copy(xn_ref.at[pl.ds(t, 1)], xs_ref.at[pl.ds(dst, 1)], sems.at[k])

    def issue(t, carry):
        for k in range(TOP_K):
            row_copy(t, k).start(priority=k % 2)
        return carry

    lax.fori_loop(0, tk, issue, 0, unroll=ROW_DMA_UNROLL)
    for k in range(TOP_K):
        pltpu.make_async_copy(xn_ref, xs_ref.at[pl.ds(0, tk)], sems.at[k]).wait()


def _dispatch(slots, xn, xs):
    n, w = xn.shape
    tk = 512
    assert n % tk == 0
    slots3 = slots.reshape(n // tk, 1, tk * TOP_K)
    kernel = functools.partial(_dispatch_kernel, tk=tk)
    return pl.pallas_call(
        kernel,
        grid=(n // tk,),
        in_specs=[
            pl.BlockSpec((1, 1, tk * TOP_K), lambda i: (i, 0, 0), memory_space=pltpu.SMEM),
            pl.BlockSpec((tk, w), lambda i: (i, 0)),
            pl.BlockSpec(memory_space=pl.ANY),
        ],
        out_specs=pl.BlockSpec(memory_space=pl.ANY),
        out_shape=jax.ShapeDtypeStruct(xs.shape, xs.dtype),
        input_output_aliases={2: 0},
        scratch_shapes=[pltpu.SemaphoreType.DMA((TOP_K,))],
        compiler_params=_cparams(("arbitrary",)),
        name="dispatch",
    )(slots3, xn, xs)


MOE_TM = 1024
MOE_SUB = 2
MOE_TF = 512


def _experts_kernel(bexp_ref, bval_ref, xs_ref, wg_ref, bg_ref, wu_ref, bu_ref, wd_ref, bd_ref,
                    ys_ref, lo_ref, hi_ref, acc_ref):
    del bexp_ref
    i = pl.program_id(0)
    j = pl.program_id(1)
    nvalid = bval_ref[i]

    @pl.when(jnp.logical_and(nvalid == 0, j == 0))
    def _():
        ys_ref[...] = jnp.zeros_like(ys_ref)

    @pl.when(nvalid > 0)
    def _():
        @pl.when(j == 0)
        def _():
            u = xs_ref[...]
            rows = lax.broadcasted_iota(I32, u.shape, 0)
            u = jnp.where(rows < nvalid, u, jnp.zeros_like(u))
            lo, hi = _unpack_halves(u)
            lo_ref[...] = lo.astype(BF16)
            hi_ref[...] = hi.astype(BF16)
            acc_ref[...] = jnp.zeros_like(acc_ref)

        half = lo_ref.shape[1]
        sub = lo_ref.shape[0] // MOE_SUB
        for h in range(MOE_SUB):
            @pl.when(nvalid > h * sub)
            def _(h=h):
                rows = pl.ds(h * sub, sub)
                lo = lo_ref[rows, :]
                hi = hi_ref[rows, :]
                hg = (jnp.dot(lo, wg_ref[0, :half, :], preferred_element_type=F32)
                      + jnp.dot(hi, wg_ref[0, half:, :], preferred_element_type=F32) + bg_ref[0])
                hu = (jnp.dot(lo, wu_ref[0, :half, :], preferred_element_type=F32)
                      + jnp.dot(hi, wu_ref[0, half:, :], preferred_element_type=F32) + bu_ref[0])
                g = jnp.minimum(hg, SWIGLU_LIMIT)
                u = jnp.clip(hu, -SWIGLU_LIMIT, SWIGLU_LIMIT)
                act = (u + 1.0) * (g * jax.nn.sigmoid(SWIGLU_ALPHA * g))
                acc_ref[rows, :] += jnp.dot(act.astype(BF16), wd_ref[0], preferred_element_type=F32)

        @pl.when(j == pl.num_programs(1) - 1)
        def _():
            ys_ref[...] = _pack_halves(acc_ref[...] + bd_ref[0])


def _experts(blk_exp, blk_valid, xs, wg, bg, wu, bu, wd, bd, tm, tf):
    p, half = xs.shape
    d = 2 * half
    dff = wg.shape[2]
    assert p % tm == 0 and dff % tf == 0
    nblk = p // tm
    nf = dff // tf

    def jeff(i, j, bval):
        return jnp.where(bval[i] > 0, j, nf - 1)

    grid_spec = pltpu.PrefetchScalarGridSpec(
        num_scalar_prefetch=2,
        grid=(nblk, nf),
        in_specs=[
            pl.BlockSpec((tm, half), lambda i, j, be, bv: (i, 0)),
            pl.BlockSpec((1, d, tf), lambda i, j, be, bv: (be[i], 0, jeff(i, j, bv))),
            pl.BlockSpec((1, 1, tf), lambda i, j, be, bv: (be[i], 0, jeff(i, j, bv))),
            pl.BlockSpec((1, d, tf), lambda i, j, be, bv: (be[i], 0, jeff(i, j, bv))),
            pl.BlockSpec((1, 1, tf), lambda i, j, be, bv: (be[i], 0, jeff(i, j, bv))),
            pl.BlockSpec((1, tf, d), lambda i, j, be, bv: (be[i], jeff(i, j, bv), 0)),
            pl.BlockSpec((1, 1, d), lambda i, j, be, bv: (be[i], 0, 0)),
        ],
        out_specs=pl.BlockSpec((tm, half), lambda i, j, be, bv: (i, 0)),
        scratch_shapes=[pltpu.VMEM((tm, half), BF16), pltpu.VMEM((tm, half), BF16),
                        pltpu.VMEM((tm, d), F32)],
    )
    return pl.pallas_call(
        _experts_kernel,
        grid_spec=grid_spec,
        out_shape=jax.ShapeDtypeStruct((p, half), U32),
        compiler_params=_cparams(("arbitrary", "arbitrary")),
        name="experts",
    )(blk_exp, blk_valid, xs, wg, bg, wu, bu, wd, bd)


def _combine_kernel(slot_ref, slot_next_ref, x1_ref, gate_ref, ys_ref, o_ref, buf_ref, sems, *, tk):
    i = pl.program_id(0)
    cur = i % 2

    def gather(slots, b):
        def issue(t, carry):
            for k in range(TOP_K):
                src = slots[0, 0, t * TOP_K + k]
                pltpu.make_async_copy(ys_ref.at[pl.ds(src, 1)], buf_ref.at[b, k, pl.ds(t, 1)],
                                      sems.at[b, k]).start(priority=k % 2)
            return carry
        lax.fori_loop(0, tk, issue, 0, unroll=ROW_DMA_UNROLL)

    @pl.when(i == 0)
    def _():
        gather(slot_ref, 0)

    @pl.when(i + 1 < pl.num_programs(0))
    def _():
        gather(slot_next_ref, 1 - cur)

    half = buf_ref.shape[3]
    gates = gate_ref[...]
    acc_lo = x1_ref[:, :half]
    acc_hi = x1_ref[:, half:]
    for k in range(TOP_K):
        pltpu.make_async_copy(ys_ref.at[pl.ds(0, tk)], buf_ref.at[cur, k], sems.at[cur, k]).wait()
        lo, hi = _unpack_halves(buf_ref[cur, k])
        g = gates[:, k:k + 1]
        acc_lo = acc_lo + g * lo
        acc_hi = acc_hi + g * hi
    o_ref[:, :half] = acc_lo
    o_ref[:, half:] = acc_hi


def _combine(slots, x1, gates, ys):
    n, d = x1.shape
    tk = 256
    assert n % tk == 0
    steps = n // tk
    slots3 = slots.reshape(steps, 1, tk * TOP_K)
    kernel = functools.partial(_combine_kernel, tk=tk)
    return pl.pallas_call(
        kernel,
        grid=(steps,),
        in_specs=[
            pl.BlockSpec((1, 1, tk * TOP_K), lambda i: (i, 0, 0), memory_space=pltpu.SMEM),
            pl.BlockSpec((1, 1, tk * TOP_K), lambda i: (jnp.minimum(i + 1, steps - 1), 0, 0),
                         memory_space=pltpu.SMEM),
            pl.BlockSpec((tk, d), lambda i: (i, 0)),
            pl.BlockSpec((tk, LANES), lambda i: (i, 0)),
            pl.BlockSpec(memory_space=pl.ANY),
        ],
        out_specs=pl.BlockSpec((tk, d), lambda i: (i, 0)),
        out_shape=jax.ShapeDtypeStruct((n, d), F32),
        scratch_shapes=[pltpu.VMEM((2, TOP_K, tk, d // 2), U32), pltpu.SemaphoreType.DMA((2, TOP_K))],
        compiler_params=_cparams(("arbitrary",)),
        name="combine",
    )(slots3, slots3, x1, gates, ys)


def _dup_kv_heads(w):
    d = w.shape[0]
    w = w.reshape(d, N_KV_B, 1, HEAD_DIM)
    return jnp.broadcast_to(w, (d, N_KV_B, 2, HEAD_DIM)).reshape(d, 2 * KV_WIDTH_B)


def kernel(x_prompt, x_sample, meta_tokens, norm_attn, w_in, q_norm_a, k_norm_a, rpb, rpb_meta, q_norm_b, k_norm_b, sinks, out_norm_a, out_norm_b, w_out, norm_mlp, w_router, b_router, w_gate, b_gate, w_up, b_up, w_down, b_down):
    depth = norm_attn.shape[0]
    assert depth == 1
    d = x_prompt.shape[-1]
    groups = [(x_prompt.reshape(-1, d), x_prompt.shape[0], x_prompt.shape[1]),
              (x_sample.reshape(-1, d), x_sample.shape[0], x_sample.shape[1])]

    wi = w_in[0]
    w_cat = jnp.concatenate([wi[:, :SEG_KB], _dup_kv_heads(wi[:, SEG_KB:SEG_KB + KV_WIDTH_B]),
                             _dup_kv_heads(wi[:, SEG_KB + KV_WIDTH_B:])], axis=1).astype(BF16)
    scale = HEAD_DIM ** -0.5
    gcol = jnp.concatenate([
        jnp.tile(q_norm_a[0].astype(F32) * scale, N_HEADS_A), jnp.tile(k_norm_a[0].astype(F32), N_HEADS_A),
        jnp.ones((WIDTH_A,), F32),
        jnp.tile(q_norm_b[0].astype(F32) * scale, N_HEADS_B), jnp.tile(k_norm_b[0].astype(F32), 2 * N_KV_B),
        jnp.ones((2 * KV_WIDTH_B,), F32)])[None, :]
    bd = jnp.asarray(np.kron(np.eye(LANES // HEAD_DIM, dtype=np.float32),
                             np.ones((HEAD_DIM, HEAD_DIM), np.float32)), BF16)
    g_attn = norm_attn[0].astype(F32)[None, :]
    bias_tab, bias_meta = _bias_tables_a(rpb[0], rpb_meta[0])
    wo = w_out[0].astype(BF16)
    ga = out_norm_a[0].astype(F32)[None, :]
    gb = out_norm_b[0].astype(F32)[None, :]
    gm = norm_mlp[0].astype(F32)[None, :]
    wr = jnp.zeros((d, LANES), F32).at[:, :N_EXPERTS].set(w_router[0].astype(F32))
    wrh = wr.astype(BF16)
    wrl = (wr - wrh.astype(F32)).astype(BF16)
    br = jnp.full((1, LANES), NEG_INF, F32).at[0, :N_EXPERTS].set(b_router[0].astype(F32))
    wg = w_gate[0].astype(BF16)
    wu = w_up[0].astype(BF16)
    wd = w_down[0].astype(BF16)
    bg = b_gate[0].astype(F32)[:, None, :]
    bu = b_up[0].astype(F32)[:, None, :]
    bdn = b_down[0].astype(F32)[:, None, :]

    meta_x = jnp.zeros((META_PAD, d), F32).at[:N_META].set(meta_tokens.astype(F32))
    meta_pos = np.minimum(np.arange(META_PAD), N_META - 1)
    meta_proj = _proj(meta_x, META_PAD, meta_pos, g_attn, w_cat, gcol, bd)
    meta_proj = jnp.where(jnp.arange(META_PAD)[:, None] < N_META, meta_proj, jnp.zeros_like(meta_proj))

    cnt = jnp.zeros((1, LANES), F32)
    staged = []
    for x2d, batch, seq_len in groups:
        pos = N_META + np.arange(seq_len)
        proj = _proj(x2d, seq_len, pos, g_attn, w_cat, gcol, bd)
        oa = _attn_a(proj, meta_proj, bias_tab, bias_meta, batch, seq_len)
        ob = _attn_b(proj, meta_proj, sinks[0], batch, seq_len)
        x1, xn, ei, gates, cnt = _post(oa, ob, x2d, ga, gb, wo, gm, wrh, wrl, br, cnt)
        staged.append((x1, xn, ei, gates))

    total = sum(g[0].shape[0] for g in groups) * TOP_K
    nblk = (total + N_EXPERTS * (MOE_TM - 1) + MOE_TM - 1) // MOE_TM
    counts = cnt[0, :N_EXPERTS].astype(I32)
    padded = (counts + MOE_TM - 1) // MOE_TM * MOE_TM
    pends = jnp.cumsum(padded)
    pstarts = pends - padded
    blk_start = jnp.arange(nblk, dtype=I32) * MOE_TM
    blk_exp = jnp.minimum(jnp.sum((blk_start[:, None] >= pends[None, :]).astype(I32), axis=1), N_EXPERTS - 1)
    blk_valid = jnp.clip(counts[blk_exp] - (blk_start - pstarts[blk_exp]), 0, MOE_TM)
    blk_valid = jnp.where(blk_start < pends[-1], blk_valid, 0).astype(I32)
    last_used = jnp.maximum(pends[-1] // MOE_TM - 1, 0)
    blk_exp = jnp.where(blk_start < pends[-1], blk_exp, blk_exp[last_used]).astype(I32)

    slots = [pstarts[ei[:, :TOP_K]] + ei[:, TOP_K:2 * TOP_K] for (_, _, ei, _) in staged]

    xs = jnp.zeros((nblk * MOE_TM, d // 2), U32)
    for (x1, xn, ei, gates), sl in zip(staged, slots):
        xs = _dispatch(sl, xn, xs)
    ys = _experts(blk_exp, blk_valid, xs, wg, bg, wu, bu, wd, bdn, MOE_TM, MOE_TF)

    outs = []
    for (x1, xn, ei, gates), sl, (x2d, batch, seq_len) in zip(staged, slots, groups):
        y = _combine(sl, x1, gates, ys)
        outs.append(y.reshape(batch, seq_len, d))
    return tuple(outs)
```

```python
import functools

import jax
import jax.numpy as jnp
import numpy as np
from jax import lax
from jax.experimental import pallas as pl
from jax.experimental.pallas import tpu as pltpu

F32 = jnp.float32
BF16 = jnp.bfloat16
U32 = jnp.uint32
I32 = jnp.int32

HEAD_DIM = 64
N_HEADS_A = 16
N_HEADS_B = 16
N_KV_B = 4
WIDTH_A = N_HEADS_A * HEAD_DIM
WIDTH_B = N_HEADS_B * HEAD_DIM
KV_WIDTH_B = N_KV_B * HEAD_DIM
GRID_W = 64
NA_KH = 8
NA_KW = 16
WINDOW = 128
WBLOCK = 128
ROT_DIM = HEAD_DIM // 4
ROPE_THETA = 500000.0
N_META = 16
N_EXPERTS = 32
TOP_K = 4
SWIGLU_LIMIT = 7.0
SWIGLU_ALPHA = 1.702
NORM_EPS = 1e-5
NEG_INF = -1e30

LANES = 128
META_PAD = LANES
VMEM_LIMIT = 56 * 1024 * 1024

PROJ_TN = 512
SEG_QA, SEG_KA, SEG_VA, SEG_QB = 0, WIDTH_A, 2 * WIDTH_A, 3 * WIDTH_A
SEG_KB = 3 * WIDTH_A + WIDTH_B
SEG_VB = SEG_KB + 2 * KV_WIDTH_B
PROJ_COLS = SEG_VB + 2 * KV_WIDTH_B


def _cparams(sem):
    return pltpu.CompilerParams(dimension_semantics=sem, vmem_limit_bytes=VMEM_LIMIT)


def _proj_kernel(x_ref, g_ref, w_ref, gcol_ref, bd_ref, c_ref, s1_ref, s2_ref, o_ref, h_ref):
    j = pl.program_id(1)

    @pl.when(j == 0)
    def _():
        x = x_ref[...]
        ms = jnp.mean(x * x, axis=-1, keepdims=True)
        h_ref[...] = (x * lax.rsqrt(ms + NORM_EPS) * g_ref[...]).astype(BF16)

    acc = jnp.dot(h_ref[...], w_ref[...], preferred_element_type=F32)
    is_v = jnp.logical_or(j == SEG_VA // PROJ_TN, j == SEG_VA // PROJ_TN + 1)
    is_v = jnp.logical_or(is_v, j == SEG_VB // PROJ_TN)
    is_rope = jnp.logical_and(j >= SEG_QB // PROJ_TN, j < SEG_VB // PROJ_TN)

    @pl.when(is_v)
    def _():
        o_ref[...] = acc.astype(o_ref.dtype)

    def head_normed(c):
        a = acc[:, c * LANES:(c + 1) * LANES]
        ssq = jnp.dot((a * a).astype(BF16), bd_ref[...], preferred_element_type=F32)
        return a * lax.rsqrt(ssq * (1.0 / HEAD_DIM) + NORM_EPS) * gcol_ref[:, c * LANES:(c + 1) * LANES]

    @pl.when(jnp.logical_and(jnp.logical_not(is_v), jnp.logical_not(is_rope)))
    def _():
        for c in range(PROJ_TN // LANES):
            o_ref[:, c * LANES:(c + 1) * LANES] = head_normed(c).astype(o_ref.dtype)

    @pl.when(is_rope)
    def _():
        for c in range(PROJ_TN // LANES):
            y = head_normed(c)
            up = pltpu.roll(y, LANES - ROT_DIM // 2, 1)
            dn = pltpu.roll(y, ROT_DIM // 2, 1)
            r = y * c_ref[...] + up * s1_ref[...] + dn * s2_ref[...]
            o_ref[:, c * LANES:(c + 1) * LANES] = r.astype(o_ref.dtype)


def _rope_tables(positions):
    inv = ROPE_THETA ** (-np.arange(0, ROT_DIM, 2, dtype=np.float32) / ROT_DIM)
    ang = positions.astype(np.float32)[:, None] * inv[None, :]
    cos, sin = np.cos(ang), np.sin(ang)
    half = ROT_DIM // 2
    n = positions.shape[0]
    c = np.ones((n, HEAD_DIM), np.float32)
    s1 = np.zeros((n, HEAD_DIM), np.float32)
    s2 = np.zeros((n, HEAD_DIM), np.float32)
    c[:, :half] = cos
    c[:, half:ROT_DIM] = cos
    s1[:, :half] = -sin
    s2[:, half:ROT_DIM] = sin
    rep = LANES // HEAD_DIM
    return tuple(jnp.asarray(np.tile(t, (1, rep))) for t in (c, s1, s2))


def _proj(x2d, seq_len, positions, g_attn, w_cat, gcol, bd):
    n, d = x2d.shape
    tm = min(1024, seq_len)
    assert seq_len % tm == 0 and n % tm == 0
    per_seq = seq_len // tm
    c, s1, s2 = _rope_tables(positions)
    grid = (n // tm, PROJ_COLS // PROJ_TN)
    tab = pl.BlockSpec((tm, LANES), lambda i, j: (i % per_seq, 0))
    return pl.pallas_call(
        _proj_kernel,
        grid=grid,
        in_specs=[
            pl.BlockSpec((tm, d), lambda i, j: (i, 0)),
            pl.BlockSpec((1, d), lambda i, j: (0, 0)),
            pl.BlockSpec((d, PROJ_TN), lambda i, j: (0, j)),
            pl.BlockSpec((1, PROJ_TN), lambda i, j: (0, j)),
            pl.BlockSpec((LANES, LANES), lambda i, j: (0, 0)),
            tab, tab, tab,
        ],
        out_specs=pl.BlockSpec((tm, PROJ_TN), lambda i, j: (i, j)),
        out_shape=jax.ShapeDtypeStruct((n, PROJ_COLS), BF16),
        scratch_shapes=[pltpu.VMEM((tm, d), BF16)],
        compiler_params=_cparams(("parallel", "arbitrary")),
        name="proj",
    )(x2d, g_attn, w_cat, gcol, bd, c, s1, s2)


ATTN_A_UNROLL = 8


def _attn_a_kernel(q_ref, k_ref, v_ref, km_ref, vm_ref, bias_ref, bmeta_ref, o_ref, *, rows):
    lane = lax.broadcasted_iota(I32, (GRID_W, LANES), 1)
    first = lane < HEAD_DIM
    km = km_ref[...]
    vm = vm_ref[...]
    bmeta = bmeta_ref[0]
    nt = (((1,), (1,)), ((), ()))

    def row_body(r, carry):
        r0 = jnp.clip(r - NA_KH // 2, 0, rows - NA_KH)
        var = r - r0
        q = q_ref[pl.ds(pl.multiple_of(r * GRID_W, GRID_W), GRID_W), :]
        zero = jnp.zeros_like(q)
        qs = jnp.concatenate([jnp.where(first, q, zero), jnp.where(first, zero, q)], axis=0)
        ks = pl.multiple_of(r0 * GRID_W, GRID_W)
        kb = k_ref[pl.ds(ks, NA_KH * GRID_W), :]
        vb = v_ref[pl.ds(ks, NA_KH * GRID_W), :]
        s = lax.dot_general(qs, kb, nt, preferred_element_type=F32) + bias_ref[0, var]
        sm = lax.dot_general(qs, km, nt, preferred_element_type=F32) + bmeta
        m = jnp.maximum(jnp.max(s, axis=-1, keepdims=True), jnp.max(sm, axis=-1, keepdims=True))
        p = jnp.exp(s - m)
        pm = jnp.exp(sm - m)
        l = jnp.sum(p, axis=-1, keepdims=True) + jnp.sum(pm, axis=-1, keepdims=True)
        o = (jnp.dot(p.astype(BF16), vb, preferred_element_type=F32)
             + jnp.dot(pm.astype(BF16), vm, preferred_element_type=F32)) / l
        o_ref[pl.ds(pl.multiple_of(r * GRID_W, GRID_W), GRID_W), :] = (
            jnp.where(first, o[:GRID_W], o[GRID_W:]).astype(o_ref.dtype))
        return carry

    lax.fori_loop(0, rows, row_body, 0, unroll=ATTN_A_UNROLL)


def _attn_a(proj, meta_proj, bias_tab, bias_meta, batch, seq_len):
    rows = seq_len // GRID_W
    assert rows >= NA_KH and rows % ATTN_A_UNROLL == 0
    n = batch * seq_len
    pairs = WIDTH_A // LANES
    kernel = functools.partial(_attn_a_kernel, rows=rows)
    return pl.pallas_call(
        kernel,
        grid=(batch, pairs),
        in_specs=[
            pl.BlockSpec((seq_len, LANES), lambda b, p: (b, SEG_QA // LANES + p)),
            pl.BlockSpec((seq_len, LANES), lambda b, p: (b, SEG_KA // LANES + p)),
            pl.BlockSpec((seq_len, LANES), lambda b, p: (b, SEG_VA // LANES + p)),
            pl.BlockSpec((META_PAD, LANES), lambda b, p: (0, SEG_KA // LANES + p)),
            pl.BlockSpec((META_PAD, LANES), lambda b, p: (0, SEG_VA // LANES + p)),
            pl.BlockSpec((1, NA_KH, 2 * GRID_W, NA_KH * GRID_W), lambda b, p: (p, 0, 0, 0)),
            pl.BlockSpec((1, 2 * GRID_W, META_PAD), lambda b, p: (p, 0, 0)),
        ],
        out_specs=pl.BlockSpec((seq_len, LANES), lambda b, p: (b, p)),
        out_shape=jax.ShapeDtypeStruct((n, WIDTH_A), BF16),
        compiler_params=_cparams(("parallel", "parallel")),
        name="attn_a",
    )(proj, proj, proj, meta_proj, meta_proj, bias_tab, bias_meta)


def _bias_tables_a(rpb, rpb_meta):
    n_dr, n_dc = 2 * NA_KH - 1, 2 * NA_KW - 1
    var = np.arange(NA_KH)[:, None]
    jj = np.arange(NA_KH)[None, :]
    ridx = jj - var + NA_KH - 1
    cq = np.arange(GRID_W)
    col_start = np.clip(cq - NA_KW // 2, 0, GRID_W - NA_KW)
    col_valid = (cq[None, :] >= col_start[:, None]) & (cq[None, :] < col_start[:, None] + NA_KW)
    cidx = np.clip(cq[None, :] - cq[:, None], -(NA_KW - 1), NA_KW - 1) + NA_KW - 1
    sel_r = (ridx[None, :, :] == np.arange(n_dr)[:, None, None]).astype(np.float32)
    sel_c = (cidx[None, :, :] == np.arange(n_dc)[:, None, None]).astype(np.float32)
    hi = lax.Precision.HIGHEST
    t1 = jnp.einsum('hab,bcw->hacw', rpb.astype(F32), jnp.asarray(sel_c), precision=hi)
    tab = jnp.einsum('avj,hacw->hvcjw', jnp.asarray(sel_r), t1, precision=hi)
    tab = jnp.where(col_valid[None, None, :, None, :], tab, NEG_INF)
    tab = tab.reshape(N_HEADS_A // 2, 2, NA_KH, GRID_W, NA_KH * GRID_W)
    tab = jnp.transpose(tab, (0, 2, 1, 3, 4)).reshape(N_HEADS_A // 2, NA_KH, 2 * GRID_W, NA_KH * GRID_W)
    bm = jnp.full((N_HEADS_A, META_PAD), NEG_INF, F32).at[:, :N_META].set(rpb_meta.astype(F32))
    bm = jnp.broadcast_to(bm[:, None, :], (N_HEADS_A, GRID_W, META_PAD)).reshape(N_HEADS_A // 2, 2 * GRID_W, META_PAD)
    return tab, bm


ATTN_B_UNROLL = 4


def _attn_b_kernel(sink_ref, q_ref, k_ref, v_ref, km_ref, vm_ref, bmeta_ref, o_ref, *, seq_len):
    kv = pl.program_id(1)
    nb = seq_len // WBLOCK
    span = 3 * WBLOCK
    group = N_HEADS_B // N_KV_B
    stack = group * WBLOCK
    lane = lax.broadcasted_iota(I32, (WBLOCK, LANES), 1)
    first = lane < HEAD_DIM
    km = km_ref[...]
    vm = vm_ref[...]
    bmeta = bmeta_ref[...]
    nt = (((1,), (1,)), ((), ()))
    qi = lax.broadcasted_iota(I32, (stack, span), 0) % WBLOCK
    kj = lax.broadcasted_iota(I32, (stack, span), 1)
    rel = kj - qi
    head_of_row = lax.broadcasted_iota(I32, (stack, 1), 0) // WBLOCK
    sink = jnp.zeros((stack, 1), F32)
    for g in range(group):
        sink = jnp.where(head_of_row == g, sink_ref[kv * group + g], sink)

    def blk_body(n, carry):
        start = jnp.clip((n - 1) * WBLOCK, 0, seq_len - span)
        start = pl.multiple_of(start, WBLOCK)
        q0 = pl.multiple_of(n * WBLOCK, WBLOCK)
        kb = k_ref[pl.ds(start, span), :]
        vb = v_ref[pl.ds(start, span), :]
        parts = []
        for c in range(group // 2):
            q = q_ref[pl.ds(q0, WBLOCK), c * LANES:(c + 1) * LANES]
            zero = jnp.zeros_like(q)
            parts += [jnp.where(first, q, zero), jnp.where(first, zero, q)]
        qs = jnp.concatenate(parts, axis=0)
        s = lax.dot_general(qs, kb, nt, preferred_element_type=F32)
        s = jnp.where(jnp.abs(rel + (start - q0)) <= WINDOW, s, NEG_INF)
        sm = lax.dot_general(qs, km, nt, preferred_element_type=F32) + bmeta
        m = jnp.maximum(jnp.max(s, axis=-1, keepdims=True), jnp.max(sm, axis=-1, keepdims=True))
        m = jnp.maximum(m, sink)
        p = jnp.exp(s - m)
        pm = jnp.exp(sm - m)
        l = jnp.sum(p, axis=-1, keepdims=True) + jnp.sum(pm, axis=-1, keepdims=True) + jnp.exp(sink - m)
        o = (jnp.dot(p.astype(BF16), vb, preferred_element_type=F32)
             + jnp.dot(pm.astype(BF16), vm, preferred_element_type=F32)) / l
        for c in range(group // 2):
            base = 2 * c * WBLOCK
            o_ref[pl.ds(q0, WBLOCK), c * LANES:(c + 1) * LANES] = jnp.where(
                first, o[base:base + WBLOCK], o[base + WBLOCK:base + 2 * WBLOCK]).astype(o_ref.dtype)
        return carry

    lax.fori_loop(0, nb, blk_body, 0, unroll=ATTN_B_UNROLL)


def _attn_b(proj, meta_proj, sinks, batch, seq_len):
    assert seq_len % (WBLOCK * ATTN_B_UNROLL) == 0 and seq_len >= 3 * WBLOCK
    n = batch * seq_len
    qw = WIDTH_B // N_KV_B
    bmeta = jnp.where(jnp.arange(META_PAD) < N_META, 0.0, NEG_INF).astype(F32)[None, :]
    kernel = functools.partial(_attn_b_kernel, seq_len=seq_len)
    grid_spec = pltpu.PrefetchScalarGridSpec(
        num_scalar_prefetch=1,
        grid=(batch, N_KV_B),
        in_specs=[
            pl.BlockSpec((seq_len, qw), lambda b, k, s: (b, SEG_QB // qw + k)),
            pl.BlockSpec((seq_len, LANES), lambda b, k, s: (b, SEG_KB // LANES + k)),
            pl.BlockSpec((seq_len, LANES), lambda b, k, s: (b, SEG_VB // LANES + k)),
            pl.BlockSpec((META_PAD, LANES), lambda b, k, s: (0, SEG_KB // LANES + k)),
            pl.BlockSpec((META_PAD, LANES), lambda b, k, s: (0, SEG_VB // LANES + k)),
            pl.BlockSpec((1, META_PAD), lambda b, k, s: (0, 0)),
        ],
        out_specs=pl.BlockSpec((seq_len, qw), lambda b, k, s: (b, k)),
    )
    return pl.pallas_call(
        kernel,
        grid_spec=grid_spec,
        out_shape=jax.ShapeDtypeStruct((n, WIDTH_B), BF16),
        compiler_params=_cparams(("parallel", "parallel")),
        name="attn_b",
    )(sinks.astype(F32), proj, proj, proj, meta_proj, meta_proj, bmeta)


def _pack_halves(a):
    w = a.shape[1] // 2
    lo = pltpu.bitcast(a[:, :w].astype(BF16).astype(F32), U32)
    hi = pltpu.bitcast(a[:, w:].astype(BF16).astype(F32), U32)
    return (lo >> 16) | (hi & jnp.uint32(0xFFFF0000))


def _unpack_halves(u):
    lo = pltpu.bitcast(u << 16, F32)
    hi = pltpu.bitcast(u & jnp.uint32(0xFFFF0000), F32)
    return lo, hi


def _post_kernel(oa_ref, ob_ref, x_ref, ga_ref, gb_ref, wo_ref, gm_ref, wrh_ref, wrl_ref, br_ref,
                 tri_ref, cnt0_ref, x1_ref, xn_ref, ei_ref, gate_ref, cnt_ref, run_ref):
    i = pl.program_id(0)

    @pl.when(i == 0)
    def _():
        run_ref[...] = cnt0_ref[...]

    def normed(ref, g_ref):
        a = ref[...].astype(F32)
        ms = jnp.mean(a * a, axis=-1, keepdims=True)
        return (a * lax.rsqrt(ms + NORM_EPS) * g_ref[...]).astype(BF16)

    wa = oa_ref.shape[1]
    mix = (jnp.dot(normed(oa_ref, ga_ref), wo_ref[:wa, :], preferred_element_type=F32)
           + jnp.dot(normed(ob_ref, gb_ref), wo_ref[wa:, :], preferred_element_type=F32))
    x1 = x_ref[...] + mix
    x1_ref[...] = x1
    ms = jnp.mean(x1 * x1, axis=-1, keepdims=True)
    xn = x1 * lax.rsqrt(ms + NORM_EPS) * gm_ref[...]
    xn_ref[...] = _pack_halves(xn)

    xh = xn.astype(BF16)
    xl = (xn - xh.astype(F32)).astype(BF16)
    logits = (jnp.dot(xh, wrh_ref[...], preferred_element_type=F32)
              + jnp.dot(xl, wrh_ref[...], preferred_element_type=F32)
              + jnp.dot(xh, wrl_ref[...], preferred_element_type=F32)) + br_ref[...]

    tm = logits.shape[0]
    lane = lax.broadcasted_iota(I32, (tm, LANES), 1)
    lanef = lane.astype(F32)
    work = logits
    vals, idxs = [], []
    chosen = jnp.zeros((tm, LANES), F32)
    for _ in range(TOP_K):
        mk = jnp.max(work, axis=-1, keepdims=True)
        ik = jnp.min(jnp.where(work == mk, lanef, float(LANES)), axis=-1, keepdims=True).astype(I32)
        hit = lane == ik
        work = jnp.where(hit, -jnp.inf, work)
        chosen = jnp.where(hit, 1.0, chosen)
        vals.append(mk)
        idxs.append(ik)
    ex = [jnp.exp(v - vals[0]) for v in vals]
    den = ex[0] + ex[1] + ex[2] + ex[3]

    prefix = jnp.dot(tri_ref[...], chosen.astype(BF16), preferred_element_type=F32) + run_ref[...]
    run_ref[...] = run_ref[...] + jnp.sum(chosen, axis=0, keepdims=True)
    cnt_ref[...] = run_ref[...]

    ei = jnp.zeros((tm, LANES), I32)
    gates = jnp.zeros((tm, LANES), F32)
    for k in range(TOP_K):
        rank = jnp.sum(jnp.where(lane == idxs[k], prefix, 0.0), axis=-1, keepdims=True).astype(I32)
        ei = jnp.where(lane == k, idxs[k], ei)
        ei = jnp.where(lane == TOP_K + k, rank, ei)
        gates = jnp.where(lane == k, ex[k] / den, gates)
    ei_ref[...] = ei
    gate_ref[...] = gates


def _post(oa, ob, x2d, ga, gb, wo, gm, wrh, wrl, br, cnt0):
    n, d = x2d.shape
    tm = 256
    assert n % tm == 0
    tri = jnp.asarray(np.tril(np.ones((tm, tm), np.float32), -1), BF16)
    row = lambda i: (i, 0)
    fixed = lambda i: (0, 0)
    return pl.pallas_call(
        _post_kernel,
        grid=(n // tm,),
        in_specs=[
            pl.BlockSpec((tm, oa.shape[1]), row),
            pl.BlockSpec((tm, ob.shape[1]), row),
            pl.BlockSpec((tm, d), row),
            pl.BlockSpec((1, oa.shape[1]), fixed),
            pl.BlockSpec((1, ob.shape[1]), fixed),
            pl.BlockSpec(wo.shape, fixed),
            pl.BlockSpec((1, d), fixed),
            pl.BlockSpec((d, LANES), fixed),
            pl.BlockSpec((d, LANES), fixed),
            pl.BlockSpec((1, LANES), fixed),
            pl.BlockSpec((tm, tm), fixed),
            pl.BlockSpec((1, LANES), fixed),
        ],
        out_specs=[
            pl.BlockSpec((tm, d), row),
            pl.BlockSpec((tm, d // 2), row),
            pl.BlockSpec((tm, LANES), row),
            pl.BlockSpec((tm, LANES), row),
            pl.BlockSpec((1, LANES), fixed),
        ],
        out_shape=[
            jax.ShapeDtypeStruct((n, d), F32),
            jax.ShapeDtypeStruct((n, d // 2), U32),
            jax.ShapeDtypeStruct((n, LANES), I32),
            jax.ShapeDtypeStruct((n, LANES), F32),
            jax.ShapeDtypeStruct((1, LANES), F32),
        ],
        scratch_shapes=[pltpu.VMEM((1, LANES), F32)],
        compiler_params=_cparams(("arbitrary",)),
        name="post",
    )(oa, ob, x2d, ga, gb, wo, gm, wrh, wrl, br, tri, cnt0)


ROW_DMA_UNROLL = 4


def _dispatch_kernel(slot_ref, xn_ref, xs_in_ref, xs_ref, sems, *, tk):
    del xs_in_ref

    def row_copy(t, k):
        dst = slot_ref[0, 0, t * TOP_K + k]
        return pltpu.make_async_copy(xn_ref.at[pl.ds(t, 1)], xs_ref.at[pl.ds(dst, 1)], sems.at[k])

    def issue(t, carry):
        for k in range(TOP_K):
            row_copy(t, k).start(priority=k % 2)
        return carry

    lax.fori_loop(0, tk, issue, 0, unroll=ROW_DMA_UNROLL)
    for k in range(TOP_K):
        pltpu.make_async_copy(xn_ref, xs_ref.at[pl.ds(0, tk)], sems.at[k]).wait()


def _dispatch(slots, xn, xs):
    n, w = xn.shape
    tk = 512
    assert n % tk == 0
    slots3 = slots.reshape(n // tk, 1, tk * TOP_K)
    kernel = functools.partial(_dispatch_kernel, tk=tk)
    return pl.pallas_call(
        kernel,
        grid=(n // tk,),
        in_specs=[
            pl.BlockSpec((1, 1, tk * TOP_K), lambda i: (i, 0, 0), memory_space=pltpu.SMEM),
            pl.BlockSpec((tk, w), lambda i: (i, 0)),
            pl.BlockSpec(memory_space=pl.ANY),
        ],
        out_specs=pl.BlockSpec(memory_space=pl.ANY),
        out_shape=jax.ShapeDtypeStruct(xs.shape, xs.dtype),
        input_output_aliases={2: 0},
        scratch_shapes=[pltpu.SemaphoreType.DMA((TOP_K,))],
        compiler_params=_cparams(("arbitrary",)),
        name="dispatch",
    )(slots3, xn, xs)


MOE_TM = 512
MOE_TF = 1024
FP8 = jnp.float8_e4m3fn
FP8_TOP = 224.0
ACT_SCALE = 4.0


def _pow2_scale(amax):
    safe = jnp.where(amax > 0, amax, FP8_TOP)
    return jnp.exp2(jnp.floor(jnp.log2(FP8_TOP / safe)))


def _fp8_expert_weights(w):
    scale = _pow2_scale(jnp.max(jnp.abs(w), axis=(1, 2), keepdims=True))
    return (w * scale).astype(FP8), (1.0 / scale).reshape(-1)


def _experts_kernel(bexp_ref, bval_ref, sg_ref, su_ref, sd_ref, xs_ref, wg_ref, bg_ref, wu_ref, bu_ref,
                    wd_ref, bd_ref, ys_ref, lo_ref, hi_ref, rinv_ref, acc_ref):
    i = pl.program_id(0)
    j = pl.program_id(1)
    nvalid = bval_ref[i]
    e = bexp_ref[i]

    @pl.when(jnp.logical_and(nvalid == 0, j == 0))
    def _():
        ys_ref[...] = jnp.zeros_like(ys_ref)

    @pl.when(nvalid > 0)
    def _():
        @pl.when(j == 0)
        def _():
            u = xs_ref[...]
            rows = lax.broadcasted_iota(I32, u.shape, 0)
            u = jnp.where(rows < nvalid, u, jnp.zeros_like(u))
            lo, hi = _unpack_halves(u)
            amax = jnp.maximum(jnp.max(jnp.abs(lo), axis=-1, keepdims=True),
                               jnp.max(jnp.abs(hi), axis=-1, keepdims=True))
            rs = _pow2_scale(amax)
            lo_ref[...] = (lo * rs).astype(FP8)
            hi_ref[...] = (hi * rs).astype(FP8)
            rinv_ref[...] = jnp.broadcast_to(1.0 / rs, rinv_ref.shape)
            acc_ref[...] = jnp.zeros_like(acc_ref)

        half = lo_ref.shape[1]
        lo = lo_ref[...]
        hi = hi_ref[...]
        rinv = rinv_ref[:, :1]
        hg = (jnp.dot(lo, wg_ref[0, :half, :], preferred_element_type=F32)
              + jnp.dot(hi, wg_ref[0, half:, :], preferred_element_type=F32)) * (rinv * sg_ref[e]) + bg_ref[0]
        hu = (jnp.dot(lo, wu_ref[0, :half, :], preferred_element_type=F32)
              + jnp.dot(hi, wu_ref[0, half:, :], preferred_element_type=F32)) * (rinv * su_ref[e]) + bu_ref[0]
        g = jnp.minimum(hg, SWIGLU_LIMIT)
        u = jnp.clip(hu, -SWIGLU_LIMIT, SWIGLU_LIMIT)
        act = (u + 1.0) * (g * jax.nn.sigmoid(SWIGLU_ALPHA * g))
        down = jnp.dot((act * ACT_SCALE).astype(FP8), wd_ref[0], preferred_element_type=F32)
        acc_ref[...] += down * (sd_ref[e] * (1.0 / ACT_SCALE))

        @pl.when(j == pl.num_programs(1) - 1)
        def _():
            ys_ref[...] = _pack_halves(acc_ref[...] + bd_ref[0])


def _experts(blk_exp, blk_valid, xs, wg, sg, bg, wu, su, bu, wd, sd, bd, tm, tf):
    p, half = xs.shape
    d = 2 * half
    dff = wg.shape[2]
    assert p % tm == 0 and dff % tf == 0
    nblk = p // tm
    nf = dff // tf

    def jeff(i, j, bval):
        return jnp.where(bval[i] > 0, j, nf - 1)

    def wspec(shape, imap):
        return pl.BlockSpec(shape, lambda i, j, be, bv, *_: imap(be[i], jeff(i, j, bv)))

    grid_spec = pltpu.PrefetchScalarGridSpec(
        num_scalar_prefetch=5,
        grid=(nblk, nf),
        in_specs=[
            pl.BlockSpec((tm, half), lambda i, j, *_: (i, 0)),
            wspec((1, d, tf), lambda e, f: (e, 0, f)),
            wspec((1, 1, tf), lambda e, f: (e, 0, f)),
            wspec((1, d, tf), lambda e, f: (e, 0, f)),
            wspec((1, 1, tf), lambda e, f: (e, 0, f)),
            wspec((1, tf, d), lambda e, f: (e, f, 0)),
            wspec((1, 1, d), lambda e, f: (e, 0, 0)),
        ],
        out_specs=pl.BlockSpec((tm, half), lambda i, j, *_: (i, 0)),
        scratch_shapes=[pltpu.VMEM((tm, half), FP8), pltpu.VMEM((tm, half), FP8),
                        pltpu.VMEM((tm, LANES), F32), pltpu.VMEM((tm, d), F32)],
    )
    return pl.pallas_call(
        _experts_kernel,
        grid_spec=grid_spec,
        out_shape=jax.ShapeDtypeStruct((p, half), U32),
        compiler_params=_cparams(("arbitrary", "arbitrary")),
        name="experts",
    )(blk_exp, blk_valid, sg, su, sd, xs, wg, bg, wu, bu, wd, bd)


def _combine_kernel(slot_ref, slot_next_ref, x1_ref, gate_ref, ys_ref, o_ref, buf_ref, sems, *, tk):
    i = pl.program_id(0)
    cur = i % 2

    def gather(slots, b):
        def issue(t, carry):
            for k in range(TOP_K):
                src = slots[0, 0, t * TOP_K + k]
                pltpu.make_async_copy(ys_ref.at[pl.ds(src, 1)], buf_ref.at[b, k, pl.ds(t, 1)],
                                      sems.at[b, k]).start(priority=k % 2)
            return carry
        lax.fori_loop(0, tk, issue, 0, unroll=ROW_DMA_UNROLL)

    @pl.when(i == 0)
    def _():
        gather(slot_ref, 0)

    @pl.when(i + 1 < pl.num_programs(0))
    def _():
        gather(slot_next_ref, 1 - cur)

    half = buf_ref.shape[3]
    gates = gate_ref[...]
    acc_lo = x1_ref[:, :half]
    acc_hi = x1_ref[:, half:]
    for k in range(TOP_K):
        pltpu.make_async_copy(ys_ref.at[pl.ds(0, tk)], buf_ref.at[cur, k], sems.at[cur, k]).wait()
        lo, hi = _unpack_halves(buf_ref[cur, k])
        g = gates[:, k:k + 1]
        acc_lo = acc_lo + g * lo
        acc_hi = acc_hi + g * hi
    o_ref[:, :half] = acc_lo
    o_ref[:, half:] = acc_hi


def _combine(slots, x1, gates, ys):
    n, d = x1.shape
    tk = 256
    assert n % tk == 0
    steps = n // tk
    slots3 = slots.reshape(steps, 1, tk * TOP_K)
    kernel = functools.partial(_combine_kernel, tk=tk)
    return pl.pallas_call(
        kernel,
        grid=(steps,),
        in_specs=[
            pl.BlockSpec((1, 1, tk * TOP_K), lambda i: (i, 0, 0), memory_space=pltpu.SMEM),
            pl.BlockSpec((1, 1, tk * TOP_K), lambda i: (jnp.minimum(i + 1, steps - 1), 0, 0),
                         memory_space=pltpu.SMEM),
            pl.BlockSpec((tk, d), lambda i: (i, 0)),
            pl.BlockSpec((tk, LANES), lambda i: (i, 0)),
            pl.BlockSpec(memory_space=pl.ANY),
        ],
        out_specs=pl.BlockSpec((tk, d), lambda i: (i, 0)),
        out_shape=jax.ShapeDtypeStruct((n, d), F32),
        scratch_shapes=[pltpu.VMEM((2, TOP_K, tk, d // 2), U32), pltpu.SemaphoreType.DMA((2, TOP_K))],
        compiler_params=_cparams(("arbitrary",)),
        name="combine",
    )(slots3, slots3, x1, gates, ys)


def _dup_kv_heads(w):
    d = w.shape[0]
    w = w.reshape(d, N_KV_B, 1, HEAD_DIM)
    return jnp.broadcast_to(w, (d, N_KV_B, 2, HEAD_DIM)).reshape(d, 2 * KV_WIDTH_B)


def kernel(x_prompt, x_sample, meta_tokens, norm_attn, w_in, q_norm_a, k_norm_a, rpb, rpb_meta, q_norm_b, k_norm_b, sinks, out_norm_a, out_norm_b, w_out, norm_mlp, w_router, b_router, w_gate, b_gate, w_up, b_up, w_down, b_down):
    depth = norm_attn.shape[0]
    assert depth == 1
    d = x_prompt.shape[-1]
    groups = [(x_prompt.reshape(-1, d), x_prompt.shape[0], x_prompt.shape[1]),
              (x_sample.reshape(-1, d), x_sample.shape[0], x_sample.shape[1])]

    wi = w_in[0]
    w_cat = jnp.concatenate([wi[:, :SEG_KB], _dup_kv_heads(wi[:, SEG_KB:SEG_KB + KV_WIDTH_B]),
                             _dup_kv_heads(wi[:, SEG_KB + KV_WIDTH_B:])], axis=1).astype(BF16)
    scale = HEAD_DIM ** -0.5
    gcol = jnp.concatenate([
        jnp.tile(q_norm_a[0].astype(F32) * scale, N_HEADS_A), jnp.tile(k_norm_a[0].astype(F32), N_HEADS_A),
        jnp.ones((WIDTH_A,), F32),
        jnp.tile(q_norm_b[0].astype(F32) * scale, N_HEADS_B), jnp.tile(k_norm_b[0].astype(F32), 2 * N_KV_B),
        jnp.ones((2 * KV_WIDTH_B,), F32)])[None, :]
    bd = jnp.asarray(np.kron(np.eye(LANES // HEAD_DIM, dtype=np.float32),
                             np.ones((HEAD_DIM, HEAD_DIM), np.float32)), BF16)
    g_attn = norm_attn[0].astype(F32)[None, :]
    bias_tab, bias_meta = _bias_tables_a(rpb[0], rpb_meta[0])
    wo = w_out[0].astype(BF16)
    ga = out_norm_a[0].astype(F32)[None, :]
    gb = out_norm_b[0].astype(F32)[None, :]
    gm = norm_mlp[0].astype(F32)[None, :]
    wr = jnp.zeros((d, LANES), F32).at[:, :N_EXPERTS].set(w_router[0].astype(F32))
    wrh = wr.astype(BF16)
    wrl = (wr - wrh.astype(F32)).astype(BF16)
    br = jnp.full((1, LANES), NEG_INF, F32).at[0, :N_EXPERTS].set(b_router[0].astype(F32))
    wg, sg = _fp8_expert_weights(w_gate[0].astype(F32))
    wu, su = _fp8_expert_weights(w_up[0].astype(F32))
    wd, sd = _fp8_expert_weights(w_down[0].astype(F32))
    bg = b_gate[0].astype(F32)[:, None, :]
    bu = b_up[0].astype(F32)[:, None, :]
    bdn = b_down[0].astype(F32)[:, None, :]

    meta_x = jnp.zeros((META_PAD, d), F32).at[:N_META].set(meta_tokens.astype(F32))
    meta_pos = np.minimum(np.arange(META_PAD), N_META - 1)
    meta_proj = _proj(meta_x, META_PAD, meta_pos, g_attn, w_cat, gcol, bd)
    meta_proj = jnp.where(jnp.arange(META_PAD)[:, None] < N_META, meta_proj, jnp.zeros_like(meta_proj))

    cnt = jnp.zeros((1, LANES), F32)
    staged = []
    for x2d, batch, seq_len in groups:
        pos = N_META + np.arange(seq_len)
        proj = _proj(x2d, seq_len, pos, g_attn, w_cat, gcol, bd)
        oa = _attn_a(proj, meta_proj, bias_tab, bias_meta, batch, seq_len)
        ob = _attn_b(proj, meta_proj, sinks[0], batch, seq_len)
        x1, xn, ei, gates, cnt = _post(oa, ob, x2d, ga, gb, wo, gm, wrh, wrl, br, cnt)
        staged.append((x1, xn, ei, gates))

    total = sum(g[0].shape[0] for g in groups) * TOP_K
    nblk = (total + N_EXPERTS * (MOE_TM - 1) + MOE_TM - 1) // MOE_TM
    counts = cnt[0, :N_EXPERTS].astype(I32)
    padded = (counts + MOE_TM - 1) // MOE_TM * MOE_TM
    pends = jnp.cumsum(padded)
    pstarts = pends - padded
    blk_start = jnp.arange(nblk, dtype=I32) * MOE_TM
    blk_exp = jnp.minimum(jnp.sum((blk_start[:, None] >= pends[None, :]).astype(I32), axis=1), N_EXPERTS - 1)
    blk_valid = jnp.clip(counts[blk_exp] - (blk_start - pstarts[blk_exp]), 0, MOE_TM)
    blk_valid = jnp.where(blk_start < pends[-1], blk_valid, 0).astype(I32)
    last_used = jnp.maximum(pends[-1] // MOE_TM - 1, 0)
    blk_exp = jnp.where(blk_start < pends[-1], blk_exp, blk_exp[last_used]).astype(I32)

    slots = [pstarts[ei[:, :TOP_K]] + ei[:, TOP_K:2 * TOP_K] for (_, _, ei, _) in staged]

    xs = jnp.zeros((nblk * MOE_TM, d // 2), U32)
    for (x1, xn, ei, gates), sl in zip(staged, slots):
        xs = _dispatch(sl, xn, xs)
    ys = _experts(blk_exp, blk_valid, xs, wg, sg, bg, wu, su, bu, wd, sd, bdn, MOE_TM, MOE_TF)

    outs = []
    for (x1, xn, ei, gates), sl, (x2d, batch, seq_len) in zip(staged, slots, groups):
        y = _combine(sl, x1, gates, ys)
        outs.append(y.reshape(batch, seq_len, d))
    return tuple(outs)
```

```python
import functools

import jax
import jax.numpy as jnp
import numpy as np
from jax import lax
from jax.experimental import pallas as pl
from jax.experimental.pallas import tpu as pltpu

F32 = jnp.float32
BF16 = jnp.bfloat16
U32 = jnp.uint32
I32 = jnp.int32

HEAD_DIM = 64
N_HEADS_A = 16
N_HEADS_B = 16
N_KV_B = 4
WIDTH_A = N_HEADS_A * HEAD_DIM
WIDTH_B = N_HEADS_B * HEAD_DIM
KV_WIDTH_B = N_KV_B * HEAD_DIM
GRID_W = 64
NA_KH = 8
NA_KW = 16
WINDOW = 128
WBLOCK = 128
ROT_DIM = HEAD_DIM // 4
ROPE_THETA = 500000.0
N_META = 16
N_EXPERTS = 32
TOP_K = 4
SWIGLU_LIMIT = 7.0
SWIGLU_ALPHA = 1.702
NORM_EPS = 1e-5
NEG_INF = -1e30

LANES = 128
META_PAD = LANES
VMEM_LIMIT = 56 * 1024 * 1024

PROJ_TN = 512
SEG_QA, SEG_KA, SEG_VA, SEG_QB = 0, WIDTH_A, 2 * WIDTH_A, 3 * WIDTH_A
SEG_KB = 3 * WIDTH_A + WIDTH_B
SEG_VB = SEG_KB + 2 * KV_WIDTH_B
PROJ_COLS = SEG_VB + 2 * KV_WIDTH_B


def _cparams(sem):
    return pltpu.CompilerParams(dimension_semantics=sem, vmem_limit_bytes=VMEM_LIMIT)


def _proj_kernel(x_ref, g_ref, w_ref, gcol_ref, bd_ref, c_ref, s1_ref, s2_ref, o_ref, h_ref):
    j = pl.program_id(1)

    @pl.when(j == 0)
    def _():
        x = x_ref[...]
        ms = jnp.mean(x * x, axis=-1, keepdims=True)
        h_ref[...] = (x * lax.rsqrt(ms + NORM_EPS) * g_ref[...]).astype(BF16)

    acc = jnp.dot(h_ref[...], w_ref[...], preferred_element_type=F32)
    is_v = jnp.logical_or(j == SEG_VA // PROJ_TN, j == SEG_VA // PROJ_TN + 1)
    is_v = jnp.logical_or(is_v, j == SEG_VB // PROJ_TN)
    is_rope = jnp.logical_and(j >= SEG_QB // PROJ_TN, j < SEG_VB // PROJ_TN)

    @pl.when(is_v)
    def _():
        o_ref[...] = acc.astype(o_ref.dtype)

    def head_normed(c):
        a = acc[:, c * LANES:(c + 1) * LANES]
        ssq = jnp.dot((a * a).astype(BF16), bd_ref[...], preferred_element_type=F32)
        return a * lax.rsqrt(ssq * (1.0 / HEAD_DIM) + NORM_EPS) * gcol_ref[:, c * LANES:(c + 1) * LANES]

    @pl.when(jnp.logical_and(jnp.logical_not(is_v), jnp.logical_not(is_rope)))
    def _():
        for c in range(PROJ_TN // LANES):
            o_ref[:, c * LANES:(c + 1) * LANES] = head_normed(c).astype(o_ref.dtype)

    @pl.when(is_rope)
    def _():
        for c in range(PROJ_TN // LANES):
            y = head_normed(c)
            up = pltpu.roll(y, LANES - ROT_DIM // 2, 1)
            dn = pltpu.roll(y, ROT_DIM // 2, 1)
            r = y * c_ref[...] + up * s1_ref[...] + dn * s2_ref[...]
            o_ref[:, c * LANES:(c + 1) * LANES] = r.astype(o_ref.dtype)


def _rope_tables(positions):
    inv = ROPE_THETA ** (-np.arange(0, ROT_DIM, 2, dtype=np.float32) / ROT_DIM)
    ang = positions.astype(np.float32)[:, None] * inv[None, :]
    cos, sin = np.cos(ang), np.sin(ang)
    half = ROT_DIM // 2
    n = positions.shape[0]
    c = np.ones((n, HEAD_DIM), np.float32)
    s1 = np.zeros((n, HEAD_DIM), np.float32)
    s2 = np.zeros((n, HEAD_DIM), np.float32)
    c[:, :half] = cos
    c[:, half:ROT_DIM] = cos
    s1[:, :half] = -sin
    s2[:, half:ROT_DIM] = sin
    rep = LANES // HEAD_DIM
    return tuple(jnp.asarray(np.tile(t, (1, rep))) for t in (c, s1, s2))


def _proj(x2d, seq_len, positions, g_attn, w_cat, gcol, bd):
    n, d = x2d.shape
    tm = min(1024, seq_len)
    assert seq_len % tm == 0 and n % tm == 0
    per_seq = seq_len // tm
    c, s1, s2 = _rope_tables(positions)
    grid = (n // tm, PROJ_COLS // PROJ_TN)
    tab = pl.BlockSpec((tm, LANES), lambda i, j: (i % per_seq, 0))
    return pl.pallas_call(
        _proj_kernel,
        grid=grid,
        in_specs=[
            pl.BlockSpec((tm, d), lambda i, j: (i, 0)),
            pl.BlockSpec((1, d), lambda i, j: (0, 0)),
            pl.BlockSpec((d, PROJ_TN), lambda i, j: (0, j)),
            pl.BlockSpec((1, PROJ_TN), lambda i, j: (0, j)),
            pl.BlockSpec((LANES, LANES), lambda i, j: (0, 0)),
            tab, tab, tab,
        ],
        out_specs=pl.BlockSpec((tm, PROJ_TN), lambda i, j: (i, j)),
        out_shape=jax.ShapeDtypeStruct((n, PROJ_COLS), BF16),
        scratch_shapes=[pltpu.VMEM((tm, d), BF16)],
        compiler_params=_cparams(("parallel", "arbitrary")),
        name="proj",
    )(x2d, g_attn, w_cat, gcol, bd, c, s1, s2)


ATTN_A_UNROLL = 8


def _attn_a_kernel(q_ref, k_ref, v_ref, km_ref, vm_ref, bias_ref, bmeta_ref, o_ref, sa_ref, sb_ref, ma_ref, mb_ref,
                   *, rows):
    lane = lax.broadcasted_iota(I32, (GRID_W, LANES), 1)
    first = lane < HEAD_DIM
    km = km_ref[...]
    vm = vm_ref[...]
    bmeta = bmeta_ref[0]
    nt = (((1,), (1,)), ((), ()))

    def band_start(r):
        return pl.multiple_of(jnp.clip(r - NA_KH // 2, 0, rows - NA_KH) * GRID_W, GRID_W)

    def scores(r, s_ref, m_ref):
        r = jnp.minimum(r, rows - 1)
        r0 = jnp.clip(r - NA_KH // 2, 0, rows - NA_KH)
        q = q_ref[pl.ds(pl.multiple_of(r * GRID_W, GRID_W), GRID_W), :]
        zero = jnp.zeros_like(q)
        qs = jnp.concatenate([jnp.where(first, q, zero), jnp.where(first, zero, q)], axis=0)
        kb = k_ref[pl.ds(band_start(r), NA_KH * GRID_W), :]
        s_ref[...] = lax.dot_general(qs, kb, nt, preferred_element_type=F32) + bias_ref[0, r - r0]
        m_ref[...] = lax.dot_general(qs, km, nt, preferred_element_type=F32) + bmeta

    def finish(r, s_ref, m_ref):
        s = s_ref[...]
        sm = m_ref[...]
        vb = v_ref[pl.ds(band_start(r), NA_KH * GRID_W), :]
        m = jnp.maximum(jnp.max(s, axis=-1, keepdims=True), jnp.max(sm, axis=-1, keepdims=True))
        p = jnp.exp(s - m)
        pm = jnp.exp(sm - m)
        l = jnp.sum(p, axis=-1, keepdims=True) + jnp.sum(pm, axis=-1, keepdims=True)
        o = (jnp.dot(p.astype(BF16), vb, preferred_element_type=F32)
             + jnp.dot(pm.astype(BF16), vm, preferred_element_type=F32)) / l
        o_ref[pl.ds(pl.multiple_of(r * GRID_W, GRID_W), GRID_W), :] = (
            jnp.where(first, o[:GRID_W], o[GRID_W:]).astype(o_ref.dtype))

    scores(0, sa_ref, ma_ref)

    def pair_body(h, carry):
        r = 2 * h
        scores(r + 1, sb_ref, mb_ref)
        finish(r, sa_ref, ma_ref)
        scores(r + 2, sa_ref, ma_ref)
        finish(r + 1, sb_ref, mb_ref)
        return carry

    lax.fori_loop(0, rows // 2, pair_body, 0, unroll=ATTN_A_UNROLL // 2)


def _attn_a(proj, meta_proj, bias_tab, bias_meta, batch, seq_len):
    rows = seq_len // GRID_W
    assert rows >= NA_KH and rows % ATTN_A_UNROLL == 0
    n = batch * seq_len
    pairs = WIDTH_A // LANES
    kernel = functools.partial(_attn_a_kernel, rows=rows)
    return pl.pallas_call(
        kernel,
        grid=(batch, pairs),
        in_specs=[
            pl.BlockSpec((seq_len, LANES), lambda b, p: (b, SEG_QA // LANES + p)),
            pl.BlockSpec((seq_len, LANES), lambda b, p: (b, SEG_KA // LANES + p)),
            pl.BlockSpec((seq_len, LANES), lambda b, p: (b, SEG_VA // LANES + p)),
            pl.BlockSpec((META_PAD, LANES), lambda b, p: (0, SEG_KA // LANES + p)),
            pl.BlockSpec((META_PAD, LANES), lambda b, p: (0, SEG_VA // LANES + p)),
            pl.BlockSpec((1, NA_KH, 2 * GRID_W, NA_KH * GRID_W), lambda b, p: (p, 0, 0, 0)),
            pl.BlockSpec((1, 2 * GRID_W, META_PAD), lambda b, p: (p, 0, 0)),
        ],
        out_specs=pl.BlockSpec((seq_len, LANES), lambda b, p: (b, p)),
        out_shape=jax.ShapeDtypeStruct((n, WIDTH_A), BF16),
        scratch_shapes=[pltpu.VMEM((2 * GRID_W, NA_KH * GRID_W), F32), pltpu.VMEM((2 * GRID_W, NA_KH * GRID_W), F32),
                        pltpu.VMEM((2 * GRID_W, META_PAD), F32), pltpu.VMEM((2 * GRID_W, META_PAD), F32)],
        compiler_params=_cparams(("parallel", "parallel")),
        name="attn_a",
    )(proj, proj, proj, meta_proj, meta_proj, bias_tab, bias_meta)


def _bias_tables_a(rpb, rpb_meta):
    n_dr, n_dc = 2 * NA_KH - 1, 2 * NA_KW - 1
    var = np.arange(NA_KH)[:, None]
    jj = np.arange(NA_KH)[None, :]
    ridx = jj - var + NA_KH - 1
    cq = np.arange(GRID_W)
    col_start = np.clip(cq - NA_KW // 2, 0, GRID_W - NA_KW)
    col_valid = (cq[None, :] >= col_start[:, None]) & (cq[None, :] < col_start[:, None] + NA_KW)
    cidx = np.clip(cq[None, :] - cq[:, None], -(NA_KW - 1), NA_KW - 1) + NA_KW - 1
    sel_r = (ridx[None, :, :] == np.arange(n_dr)[:, None, None]).astype(np.float32)
    sel_c = (cidx[None, :, :] == np.arange(n_dc)[:, None, None]).astype(np.float32)
    hi = lax.Precision.HIGHEST
    t1 = jnp.einsum('hab,bcw->hacw', rpb.astype(F32), jnp.asarray(sel_c), precision=hi)
    tab = jnp.einsum('avj,hacw->hvcjw', jnp.asarray(sel_r), t1, precision=hi)
    tab = jnp.where(col_valid[None, None, :, None, :], tab, NEG_INF)
    tab = tab.reshape(N_HEADS_A // 2, 2, NA_KH, GRID_W, NA_KH * GRID_W)
    tab = jnp.transpose(tab, (0, 2, 1, 3, 4)).reshape(N_HEADS_A // 2, NA_KH, 2 * GRID_W, NA_KH * GRID_W)
    bm = jnp.full((N_HEADS_A, META_PAD), NEG_INF, F32).at[:, :N_META].set(rpb_meta.astype(F32))
    bm = jnp.broadcast_to(bm[:, None, :], (N_HEADS_A, GRID_W, META_PAD)).reshape(N_HEADS_A // 2, 2 * GRID_W, META_PAD)
    return tab, bm


ATTN_B_UNROLL = 2


def _attn_b_kernel(sink_ref, q_ref, k_ref, v_ref, km_ref, vm_ref, bmeta_ref, mask_ref, o_ref, sa_ref, sb_ref,
                   ma_ref, mb_ref, *, seq_len):
    kv = pl.program_id(1)
    nb = seq_len // WBLOCK
    span = 3 * WBLOCK
    group = N_HEADS_B // N_KV_B
    stack = group * WBLOCK
    lane = lax.broadcasted_iota(I32, (WBLOCK, LANES), 1)
    first = lane < HEAD_DIM
    km = km_ref[...]
    vm = vm_ref[...]
    bmeta = bmeta_ref[...]
    nt = (((1,), (1,)), ((), ()))
    head_of_row = lax.broadcasted_iota(I32, (stack, 1), 0) // WBLOCK
    sink = jnp.zeros((stack, 1), F32)
    for g in range(group):
        sink = jnp.where(head_of_row == g, sink_ref[kv * group + g], sink)

    def band_start(n):
        return pl.multiple_of(jnp.clip((n - 1) * WBLOCK, 0, seq_len - span), WBLOCK)

    def scores(n, s_ref, m_ref):
        n = jnp.minimum(n, nb - 1)
        start = band_start(n)
        q0 = pl.multiple_of(n * WBLOCK, WBLOCK)
        kb = k_ref[pl.ds(start, span), :]
        parts = []
        for c in range(group // 2):
            q = q_ref[pl.ds(q0, WBLOCK), c * LANES:(c + 1) * LANES]
            zero = jnp.zeros_like(q)
            parts += [jnp.where(first, q, zero), jnp.where(first, zero, q)]
        qs = jnp.concatenate(parts, axis=0)
        place = (q0 - start) // WBLOCK
        s_ref[...] = lax.dot_general(qs, kb, nt, preferred_element_type=F32) + mask_ref[place]
        m_ref[...] = lax.dot_general(qs, km, nt, preferred_element_type=F32) + bmeta

    def finish(n, s_ref, m_ref):
        s = s_ref[...]
        sm = m_ref[...]
        q0 = pl.multiple_of(n * WBLOCK, WBLOCK)
        vb = v_ref[pl.ds(band_start(n), span), :]
        m = jnp.maximum(jnp.max(s, axis=-1, keepdims=True), jnp.max(sm, axis=-1, keepdims=True))
        m = jnp.maximum(m, sink)
        p = jnp.exp(s - m)
        pm = jnp.exp(sm - m)
        l = jnp.sum(p, axis=-1, keepdims=True) + jnp.sum(pm, axis=-1, keepdims=True) + jnp.exp(sink - m)
        o = (jnp.dot(p.astype(BF16), vb, preferred_element_type=F32)
             + jnp.dot(pm.astype(BF16), vm, preferred_element_type=F32)) / l
        for c in range(group // 2):
            base = 2 * c * WBLOCK
            o_ref[pl.ds(q0, WBLOCK), c * LANES:(c + 1) * LANES] = jnp.where(
                first, o[base:base + WBLOCK], o[base + WBLOCK:base + 2 * WBLOCK]).astype(o_ref.dtype)

    scores(0, sa_ref, ma_ref)

    def pair_body(h, carry):
        n = 2 * h
        scores(n + 1, sb_ref, mb_ref)
        finish(n, sa_ref, ma_ref)
        scores(n + 2, sa_ref, ma_ref)
        finish(n + 1, sb_ref, mb_ref)
        return carry

    lax.fori_loop(0, nb // 2, pair_body, 0, unroll=ATTN_B_UNROLL // 2)


def _attn_b(proj, meta_proj, sinks, batch, seq_len):
    assert seq_len % (WBLOCK * ATTN_B_UNROLL) == 0 and seq_len >= 3 * WBLOCK
    n = batch * seq_len
    qw = WIDTH_B // N_KV_B
    stack = N_HEADS_B // N_KV_B * WBLOCK
    bmeta = jnp.where(jnp.arange(META_PAD) < N_META, 0.0, NEG_INF).astype(F32)[None, :]
    qi = np.arange(WBLOCK)[None, :, None]
    kj = np.arange(3 * WBLOCK)[None, None, :]
    place = np.arange(3)[:, None, None]
    window = np.where(np.abs(kj - (place * WBLOCK + qi)) <= WINDOW, 0.0, NEG_INF).astype(np.float32)
    window = jnp.asarray(np.tile(window, (1, N_HEADS_B // N_KV_B, 1)))
    kernel = functools.partial(_attn_b_kernel, seq_len=seq_len)
    grid_spec = pltpu.PrefetchScalarGridSpec(
        num_scalar_prefetch=1,
        grid=(batch, N_KV_B),
        in_specs=[
            pl.BlockSpec((seq_len, qw), lambda b, k, s: (b, SEG_QB // qw + k)),
            pl.BlockSpec((seq_len, LANES), lambda b, k, s: (b, SEG_KB // LANES + k)),
            pl.BlockSpec((seq_len, LANES), lambda b, k, s: (b, SEG_VB // LANES + k)),
            pl.BlockSpec((META_PAD, LANES), lambda b, k, s: (0, SEG_KB // LANES + k)),
            pl.BlockSpec((META_PAD, LANES), lambda b, k, s: (0, SEG_VB // LANES + k)),
            pl.BlockSpec((1, META_PAD), lambda b, k, s: (0, 0)),
            pl.BlockSpec((3, stack, 3 * WBLOCK), lambda b, k, s: (0, 0, 0)),
        ],
        out_specs=pl.BlockSpec((seq_len, qw), lambda b, k, s: (b, k)),
        scratch_shapes=[pltpu.VMEM((stack, 3 * WBLOCK), F32), pltpu.VMEM((stack, 3 * WBLOCK), F32),
                        pltpu.VMEM((stack, META_PAD), F32), pltpu.VMEM((stack, META_PAD), F32)],
    )
    return pl.pallas_call(
        kernel,
        grid_spec=grid_spec,
        out_shape=jax.ShapeDtypeStruct((n, WIDTH_B), BF16),
        compiler_params=_cparams(("parallel", "parallel")),
        name="attn_b",
    )(sinks.astype(F32), proj, proj, proj, meta_proj, meta_proj, bmeta, window)


def _pack_halves(a):
    w = a.shape[1] // 2
    lo = pltpu.bitcast(a[:, :w].astype(BF16).astype(F32), U32)
    hi = pltpu.bitcast(a[:, w:].astype(BF16).astype(F32), U32)
    return (lo >> 16) | (hi & jnp.uint32(0xFFFF0000))


def _unpack_halves(u):
    lo = pltpu.bitcast(u << 16, F32)
    hi = pltpu.bitcast(u & jnp.uint32(0xFFFF0000), F32)
    return lo, hi


def _post_kernel(oa_ref, ob_ref, x_ref, ga_ref, gb_ref, wo_ref, gm_ref, wrh_ref, wrl_ref, br_ref,
                 tri_ref, cnt0_ref, x1_ref, xn_ref, ei_ref, gate_ref, cnt_ref, run_ref):
    i = pl.program_id(0)

    @pl.when(i == 0)
    def _():
        run_ref[...] = cnt0_ref[...]

    def normed(ref, g_ref):
        a = ref[...].astype(F32)
        ms = jnp.mean(a * a, axis=-1, keepdims=True)
        return (a * lax.rsqrt(ms + NORM_EPS) * g_ref[...]).astype(BF16)

    wa = oa_ref.shape[1]
    mix = (jnp.dot(normed(oa_ref, ga_ref), wo_ref[:wa, :], preferred_element_type=F32)
           + jnp.dot(normed(ob_ref, gb_ref), wo_ref[wa:, :], preferred_element_type=F32))
    x1 = x_ref[...] + mix
    x1_ref[...] = x1
    ms = jnp.mean(x1 * x1, axis=-1, keepdims=True)
    xn = x1 * lax.rsqrt(ms + NORM_EPS) * gm_ref[...]
    xn_ref[...] = _pack_halves(xn)

    xh = xn.astype(BF16)
    xl = (xn - xh.astype(F32)).astype(BF16)
    logits = (jnp.dot(xh, wrh_ref[...], preferred_element_type=F32)
              + jnp.dot(xl, wrh_ref[...], preferred_element_type=F32)
              + jnp.dot(xh, wrl_ref[...], preferred_element_type=F32)) + br_ref[...]

    tm = logits.shape[0]
    lane = lax.broadcasted_iota(I32, (tm, LANES), 1)
    lanef = lane.astype(F32)
    work = logits
    vals, idxs = [], []
    chosen = jnp.zeros((tm, LANES), F32)
    for _ in range(TOP_K):
        mk = jnp.max(work, axis=-1, keepdims=True)
        ik = jnp.min(jnp.where(work == mk, lanef, float(LANES)), axis=-1, keepdims=True).astype(I32)
        hit = lane == ik
        work = jnp.where(hit, -jnp.inf, work)
        chosen = jnp.where(hit, 1.0, chosen)
        vals.append(mk)
        idxs.append(ik)
    ex = [jnp.exp(v - vals[0]) for v in vals]
    den = ex[0] + ex[1] + ex[2] + ex[3]

    prefix = jnp.dot(tri_ref[...], chosen.astype(BF16), preferred_element_type=F32) + run_ref[...]
    run_ref[...] = run_ref[...] + jnp.sum(chosen, axis=0, keepdims=True)
    cnt_ref[...] = run_ref[...]

    ei = jnp.zeros((tm, LANES), I32)
    gates = jnp.zeros((tm, LANES), F32)
    for k in range(TOP_K):
        rank = jnp.sum(jnp.where(lane == idxs[k], prefix, 0.0), axis=-1, keepdims=True).astype(I32)
        ei = jnp.where(lane == k, idxs[k], ei)
        ei = jnp.where(lane == TOP_K + k, rank, ei)
        gates = jnp.where(lane == k, ex[k] / den, gates)
    ei_ref[...] = ei
    gate_ref[...] = gates


def _post(oa, ob, x2d, ga, gb, wo, gm, wrh, wrl, br, cnt0):
    n, d = x2d.shape
    tm = 256
    assert n % tm == 0
    tri = jnp.asarray(np.tril(np.ones((tm, tm), np.float32), -1), BF16)
    row = lambda i: (i, 0)
    fixed = lambda i: (0, 0)
    return pl.pallas_call(
        _post_kernel,
        grid=(n // tm,),
        in_specs=[
            pl.BlockSpec((tm, oa.shape[1]), row),
            pl.BlockSpec((tm, ob.shape[1]), row),
            pl.BlockSpec((tm, d), row),
            pl.BlockSpec((1, oa.shape[1]), fixed),
            pl.BlockSpec((1, ob.shape[1]), fixed),
            pl.BlockSpec(wo.shape, fixed),
            pl.BlockSpec((1, d), fixed),
            pl.BlockSpec((d, LANES), fixed),
            pl.BlockSpec((d, LANES), fixed),
            pl.BlockSpec((1, LANES), fixed),
            pl.BlockSpec((tm, tm), fixed),
            pl.BlockSpec((1, LANES), fixed),
        ],
        out_specs=[
            pl.BlockSpec((tm, d), row),
            pl.BlockSpec((tm, d // 2), row),
            pl.BlockSpec((tm, LANES), row),
            pl.BlockSpec((tm, LANES), row),
            pl.BlockSpec((1, LANES), fixed),
        ],
        out_shape=[
            jax.ShapeDtypeStruct((n, d), F32),
            jax.ShapeDtypeStruct((n, d // 2), U32),
            jax.ShapeDtypeStruct((n, LANES), I32),
            jax.ShapeDtypeStruct((n, LANES), F32),
            jax.ShapeDtypeStruct((1, LANES), F32),
        ],
        scratch_shapes=[pltpu.VMEM((1, LANES), F32)],
        compiler_params=_cparams(("arbitrary",)),
        name="post",
    )(oa, ob, x2d, ga, gb, wo, gm, wrh, wrl, br, tri, cnt0)


ROW_DMA_UNROLL = 4


def _dispatch_kernel(slot_ref, xn_ref, xs_in_ref, xs_ref, sems, *, tk):
    del xs_in_ref

    def row_copy(t, k):
        dst = slot_ref[0, 0, t * TOP_K + k]
        return pltpu.make_async_copy(xn_ref.at[pl.ds(t, 1)], xs_ref.at[pl.ds(dst, 1)], sems.at[k])

    def issue(t, carry):
        for k in range(TOP_K):
            row_copy(t, k).start(priority=k % 2)
        return carry

    lax.fori_loop(0, tk, issue, 0, unroll=ROW_DMA_UNROLL)
    for k in range(TOP_K):
        pltpu.make_async_copy(xn_ref, xs_ref.at[pl.ds(0, tk)], sems.at[k]).wait()


def _dispatch(slots, xn, xs):
    n, w = xn.shape
    tk = 512
    assert n % tk == 0
    slots3 = slots.reshape(n // tk, 1, tk * TOP_K)
    kernel = functools.partial(_dispatch_kernel, tk=tk)
    return pl.pallas_call(
        kernel,
        grid=(n // tk,),
        in_specs=[
            pl.BlockSpec((1, 1, tk * TOP_K), lambda i: (i, 0, 0), memory_space=pltpu.SMEM),
            pl.BlockSpec((tk, w), lambda i: (i, 0)),
            pl.BlockSpec(memory_space=pl.ANY),
        ],
        out_specs=pl.BlockSpec(memory_space=pl.ANY),
        out_shape=jax.ShapeDtypeStruct(xs.shape, xs.dtype),
        input_output_aliases={2: 0},
        scratch_shapes=[pltpu.SemaphoreType.DMA((TOP_K,))],
        compiler_params=_cparams(("arbitrary",)),
        name="dispatch",
    )(slots3, xn, xs)


MOE_TM = 512
MOE_TF = 1024
FP8 = jnp.float8_e4m3fn
FP8_TOP = 224.0
ACT_SCALE = 4.0


def _pow2_scale(amax):
    safe = jnp.where(amax > 0, amax, FP8_TOP)
    return jnp.exp2(jnp.floor(jnp.log2(FP8_TOP / safe)))


def _quantize_kernel(w_ref, q_ref, s_ref):
    w = w_ref[0]
    scale = _pow2_scale(jnp.max(jnp.abs(w), axis=(0, 1), keepdims=True))
    q_ref[0] = (w * scale).astype(FP8)
    s_ref[...] = jnp.broadcast_to(1.0 / scale, s_ref.shape)


def _fp8_expert_weights(w, block):
    n_e, a, b = w.shape
    ta, tb = block
    assert a % ta == 0 and b % tb == 0 and (a == ta or b == tb)
    tiles = (a // ta) * (b // tb)
    tile_index = (lambda e, t: (e, t, 0)) if b == tb else (lambda e, t: (e, 0, t))
    q, s = pl.pallas_call(
        _quantize_kernel,
        grid=(n_e, tiles),
        in_specs=[pl.BlockSpec((1, ta, tb), tile_index)],
        out_specs=[pl.BlockSpec((1, ta, tb), tile_index),
                   pl.BlockSpec((1, 1, 8, LANES), lambda e, t: (e, t, 0, 0))],
        out_shape=[jax.ShapeDtypeStruct(w.shape, FP8), jax.ShapeDtypeStruct((n_e, tiles, 8, LANES), F32)],
        compiler_params=_cparams(("parallel", "parallel")),
        name="quantize",
    )(w)
    return q, s[:, :, 0, 0].reshape(-1)


def _experts_kernel(bexp_ref, bval_ref, sg_ref, su_ref, sd_ref, xs_ref, wg_ref, bg_ref, wu_ref, bu_ref,
                    wd_ref, bd_ref, ys_ref, lo_ref, hi_ref, rinv_ref, acc_ref):
    i = pl.program_id(0)
    j = pl.program_id(1)
    nvalid = bval_ref[i]
    tile = bexp_ref[i] * pl.num_programs(1) + j

    @pl.when(jnp.logical_and(nvalid == 0, j == 0))
    def _():
        ys_ref[...] = jnp.zeros_like(ys_ref)

    @pl.when(nvalid > 0)
    def _():
        @pl.when(j == 0)
        def _():
            u = xs_ref[...]
            rows = lax.broadcasted_iota(I32, u.shape, 0)
            u = jnp.where(rows < nvalid, u, jnp.zeros_like(u))
            lo, hi = _unpack_halves(u)
            amax = jnp.maximum(jnp.max(jnp.abs(lo), axis=-1, keepdims=True),
                               jnp.max(jnp.abs(hi), axis=-1, keepdims=True))
            rs = _pow2_scale(amax)
            lo_ref[...] = (lo * rs).astype(FP8)
            hi_ref[...] = (hi * rs).astype(FP8)
            rinv_ref[...] = jnp.broadcast_to(1.0 / rs, rinv_ref.shape)
            acc_ref[...] = jnp.zeros_like(acc_ref)

        half = lo_ref.shape[1]
        lo = lo_ref[...]
        hi = hi_ref[...]
        rinv = rinv_ref[:, :1]
        hg = (jnp.dot(lo, wg_ref[0, :half, :], preferred_element_type=F32)
              + jnp.dot(hi, wg_ref[0, half:, :], preferred_element_type=F32)) * (rinv * sg_ref[tile]) + bg_ref[0]
        hu = (jnp.dot(lo, wu_ref[0, :half, :], preferred_element_type=F32)
              + jnp.dot(hi, wu_ref[0, half:, :], preferred_element_type=F32)) * (rinv * su_ref[tile]) + bu_ref[0]
        g = jnp.minimum(hg, SWIGLU_LIMIT)
        u = jnp.clip(hu, -SWIGLU_LIMIT, SWIGLU_LIMIT)
        act = (u + 1.0) * (g * jax.nn.sigmoid(SWIGLU_ALPHA * g))
        down = jnp.dot((act * ACT_SCALE).astype(FP8), wd_ref[0], preferred_element_type=F32)
        acc_ref[...] += down * (sd_ref[tile] * (1.0 / ACT_SCALE))

        @pl.when(j == pl.num_programs(1) - 1)
        def _():
            ys_ref[...] = _pack_halves(acc_ref[...] + bd_ref[0])


def _experts(blk_exp, blk_valid, xs, wg, sg, bg, wu, su, bu, wd, sd, bd, tm, tf):
    p, half = xs.shape
    d = 2 * half
    dff = wg.shape[2]
    assert p % tm == 0 and dff % tf == 0
    nblk = p // tm
    nf = dff // tf

    def jeff(i, j, bval):
        return jnp.where(bval[i] > 0, j, nf - 1)

    def wspec(shape, imap):
        return pl.BlockSpec(shape, lambda i, j, be, bv, *_: imap(be[i], jeff(i, j, bv)))

    grid_spec = pltpu.PrefetchScalarGridSpec(
        num_scalar_prefetch=5,
        grid=(nblk, nf),
        in_specs=[
            pl.BlockSpec((tm, half), lambda i, j, *_: (i, 0)),
            wspec((1, d, tf), lambda e, f: (e, 0, f)),
            wspec((1, 1, tf), lambda e, f: (e, 0, f)),
            wspec((1, d, tf), lambda e, f: (e, 0, f)),
            wspec((1, 1, tf), lambda e, f: (e, 0, f)),
            wspec((1, tf, d), lambda e, f: (e, f, 0)),
            wspec((1, 1, d), lambda e, f: (e, 0, 0)),
        ],
        out_specs=pl.BlockSpec((tm, half), lambda i, j, *_: (i, 0)),
        scratch_shapes=[pltpu.VMEM((tm, half), FP8), pltpu.VMEM((tm, half), FP8),
                        pltpu.VMEM((tm, LANES), F32), pltpu.VMEM((tm, d), F32)],
    )
    return pl.pallas_call(
        _experts_kernel,
        grid_spec=grid_spec,
        out_shape=jax.ShapeDtypeStruct((p, half), U32),
        compiler_params=_cparams(("arbitrary", "arbitrary")),
        name="experts",
    )(blk_exp, blk_valid, sg, su, sd, xs, wg, bg, wu, bu, wd, bd)


def _combine_kernel(slot_ref, slot_next_ref, x1_ref, gate_ref, ys_ref, o_ref, buf_ref, sems, *, tk):
    i = pl.program_id(0)
    cur = i % 2

    def gather(slots, b):
        def issue(t, carry):
            for k in range(TOP_K):
                src = slots[0, 0, t * TOP_K + k]
                pltpu.make_async_copy(ys_ref.at[pl.ds(src, 1)], buf_ref.at[b, k, pl.ds(t, 1)],
                                      sems.at[b, k]).start(priority=k % 2)
            return carry
        lax.fori_loop(0, tk, issue, 0, unroll=ROW_DMA_UNROLL)

    @pl.when(i == 0)
    def _():
        gather(slot_ref, 0)

    @pl.when(i + 1 < pl.num_programs(0))
    def _():
        gather(slot_next_ref, 1 - cur)

    half = buf_ref.shape[3]
    gates = gate_ref[...]
    acc_lo = x1_ref[:, :half]
    acc_hi = x1_ref[:, half:]
    for k in range(TOP_K):
        pltpu.make_async_copy(ys_ref.at[pl.ds(0, tk)], buf_ref.at[cur, k], sems.at[cur, k]).wait()
        lo, hi = _unpack_halves(buf_ref[cur, k])
        g = gates[:, k:k + 1]
        acc_lo = acc_lo + g * lo
        acc_hi = acc_hi + g * hi
    o_ref[:, :half] = acc_lo
    o_ref[:, half:] = acc_hi


def _combine(slots, x1, gates, ys):
    n, d = x1.shape
    tk = 256
    assert n % tk == 0
    steps = n // tk
    slots3 = slots.reshape(steps, 1, tk * TOP_K)
    kernel = functools.partial(_combine_kernel, tk=tk)
    return pl.pallas_call(
        kernel,
        grid=(steps,),
        in_specs=[
            pl.BlockSpec((1, 1, tk * TOP_K), lambda i: (i, 0, 0), memory_space=pltpu.SMEM),
            pl.BlockSpec((1, 1, tk * TOP_K), lambda i: (jnp.minimum(i + 1, steps - 1), 0, 0),
                         memory_space=pltpu.SMEM),
            pl.BlockSpec((tk, d), lambda i: (i, 0)),
            pl.BlockSpec((tk, LANES), lambda i: (i, 0)),
            pl.BlockSpec(memory_space=pl.ANY),
        ],
        out_specs=pl.BlockSpec((tk, d), lambda i: (i, 0)),
        out_shape=jax.ShapeDtypeStruct((n, d), F32),
        scratch_shapes=[pltpu.VMEM((2, TOP_K, tk, d // 2), U32), pltpu.SemaphoreType.DMA((2, TOP_K))],
        compiler_params=_cparams(("arbitrary",)),
        name="combine",
    )(slots3, slots3, x1, gates, ys)


def _dup_kv_heads(w):
    d = w.shape[0]
    w = w.reshape(d, N_KV_B, 1, HEAD_DIM)
    return jnp.broadcast_to(w, (d, N_KV_B, 2, HEAD_DIM)).reshape(d, 2 * KV_WIDTH_B)


def kernel(x_prompt, x_sample, meta_tokens, norm_attn, w_in, q_norm_a, k_norm_a, rpb, rpb_meta, q_norm_b, k_norm_b, sinks, out_norm_a, out_norm_b, w_out, norm_mlp, w_router, b_router, w_gate, b_gate, w_up, b_up, w_down, b_down):
    depth = norm_attn.shape[0]
    assert depth == 1
    d = x_prompt.shape[-1]
    groups = [(x_prompt.reshape(-1, d), x_prompt.shape[0], x_prompt.shape[1]),
              (x_sample.reshape(-1, d), x_sample.shape[0], x_sample.shape[1])]

    wi = w_in[0]
    w_cat = jnp.concatenate([wi[:, :SEG_KB], _dup_kv_heads(wi[:, SEG_KB:SEG_KB + KV_WIDTH_B]),
                             _dup_kv_heads(wi[:, SEG_KB + KV_WIDTH_B:])], axis=1).astype(BF16)
    scale = HEAD_DIM ** -0.5
    gcol = jnp.concatenate([
        jnp.tile(q_norm_a[0].astype(F32) * scale, N_HEADS_A), jnp.tile(k_norm_a[0].astype(F32), N_HEADS_A),
        jnp.ones((WIDTH_A,), F32),
        jnp.tile(q_norm_b[0].astype(F32) * scale, N_HEADS_B), jnp.tile(k_norm_b[0].astype(F32), 2 * N_KV_B),
        jnp.ones((2 * KV_WIDTH_B,), F32)])[None, :]
    bd = jnp.asarray(np.kron(np.eye(LANES // HEAD_DIM, dtype=np.float32),
                             np.ones((HEAD_DIM, HEAD_DIM), np.float32)), BF16)
    g_attn = norm_attn[0].astype(F32)[None, :]
    bias_tab, bias_meta = _bias_tables_a(rpb[0], rpb_meta[0])
    wo = w_out[0].astype(BF16)
    ga = out_norm_a[0].astype(F32)[None, :]
    gb = out_norm_b[0].astype(F32)[None, :]
    gm = norm_mlp[0].astype(F32)[None, :]
    wr = jnp.zeros((d, LANES), F32).at[:, :N_EXPERTS].set(w_router[0].astype(F32))
    wrh = wr.astype(BF16)
    wrl = (wr - wrh.astype(F32)).astype(BF16)
    br = jnp.full((1, LANES), NEG_INF, F32).at[0, :N_EXPERTS].set(b_router[0].astype(F32))
    wg, sg = _fp8_expert_weights(w_gate[0].astype(F32), (d, MOE_TF))
    wu, su = _fp8_expert_weights(w_up[0].astype(F32), (d, MOE_TF))
    wd, sd = _fp8_expert_weights(w_down[0].astype(F32), (MOE_TF, d))
    bg = b_gate[0].astype(F32)[:, None, :]
    bu = b_up[0].astype(F32)[:, None, :]
    bdn = b_down[0].astype(F32)[:, None, :]

    meta_x = jnp.zeros((META_PAD, d), F32).at[:N_META].set(meta_tokens.astype(F32))
    meta_pos = np.minimum(np.arange(META_PAD), N_META - 1)
    meta_proj = _proj(meta_x, META_PAD, meta_pos, g_attn, w_cat, gcol, bd)
    meta_proj = jnp.where(jnp.arange(META_PAD)[:, None] < N_META, meta_proj, jnp.zeros_like(meta_proj))

    cnt = jnp.zeros((1, LANES), F32)
    staged = []
    for x2d, batch, seq_len in groups:
        pos = N_META + np.arange(seq_len)
        proj = _proj(x2d, seq_len, pos, g_attn, w_cat, gcol, bd)
        oa = _attn_a(proj, meta_proj, bias_tab, bias_meta, batch, seq_len)
        ob = _attn_b(proj, meta_proj, sinks[0], batch, seq_len)
        x1, xn, ei, gates, cnt = _post(oa, ob, x2d, ga, gb, wo, gm, wrh, wrl, br, cnt)
        staged.append((x1, xn, ei, gates))

    total = sum(g[0].shape[0] for g in groups) * TOP_K
    nblk = (total + N_EXPERTS * (MOE_TM - 1) + MOE_TM - 1) // MOE_TM
    counts = cnt[0, :N_EXPERTS].astype(I32)
    padded = (counts + MOE_TM - 1) // MOE_TM * MOE_TM
    pends = jnp.cumsum(padded)
    pstarts = pends - padded
    blk_start = jnp.arange(nblk, dtype=I32) * MOE_TM
    blk_exp = jnp.minimum(jnp.sum((blk_start[:, None] >= pends[None, :]).astype(I32), axis=1), N_EXPERTS - 1)
    blk_valid = jnp.clip(counts[blk_exp] - (blk_start - pstarts[blk_exp]), 0, MOE_TM)
    blk_valid = jnp.where(blk_start < pends[-1], blk_valid, 0).astype(I32)
    last_used = jnp.maximum(pends[-1] // MOE_TM - 1, 0)
    blk_exp = jnp.where(blk_start < pends[-1], blk_exp, blk_exp[last_used]).astype(I32)

    slots = [pstarts[ei[:, :TOP_K]] + ei[:, TOP_K:2 * TOP_K] for (_, _, ei, _) in staged]

    xs = jnp.zeros((nblk * MOE_TM, d // 2), U32)
    for (x1, xn, ei, gates), sl in zip(staged, slots):
        xs = _dispatch(sl, xn, xs)
    ys = _experts(blk_exp, blk_valid, xs, wg, sg, bg, wu, su, bu, wd, sd, bdn, MOE_TM, MOE_TF)

    outs = []
    for (x1, xn, ei, gates), sl, (x2d, batch, seq_len) in zip(staged, slots, groups):
        y = _combine(sl, x1, gates, ys)
        outs.append(y.reshape(batch, seq_len, d))
    return tuple(outs)
```

```python
import functools

import jax
import jax.numpy as jnp
import numpy as np
from jax import lax
from jax.experimental import pallas as pl
from jax.experimental.pallas import tpu as pltpu

F32 = jnp.float32
BF16 = jnp.bfloat16
U32 = jnp.uint32
I32 = jnp.int32

HEAD_DIM = 64
N_HEADS_A = 16
N_HEADS_B = 16
N_KV_B = 4
WIDTH_A = N_HEADS_A * HEAD_DIM
WIDTH_B = N_HEADS_B * HEAD_DIM
KV_WIDTH_B = N_KV_B * HEAD_DIM
GRID_W = 64
NA_KH = 8
NA_KW = 16
WINDOW = 128
WBLOCK = 128
ROT_DIM = HEAD_DIM // 4
ROPE_THETA = 500000.0
N_META = 16
N_EXPERTS = 32
TOP_K = 4
SWIGLU_LIMIT = 7.0
SWIGLU_ALPHA = 1.702
NORM_EPS = 1e-5
NEG_INF = -1e30

LANES = 128
META_PAD = LANES
VMEM_LIMIT = 56 * 1024 * 1024

PROJ_TN = 1024
PROJ_HALF = PROJ_TN // 2
PROJ_TILE_KINDS = (("norm", "norm"), ("norm", "norm"), ("plain", "plain"), ("rope", "rope"), ("rope", "plain"))
SEG_QA, SEG_KA, SEG_VA, SEG_QB = 0, WIDTH_A, 2 * WIDTH_A, 3 * WIDTH_A
SEG_KB = 3 * WIDTH_A + WIDTH_B
SEG_VB = SEG_KB + 2 * KV_WIDTH_B
PROJ_COLS = SEG_VB + 2 * KV_WIDTH_B


def _cparams(sem):
    return pltpu.CompilerParams(dimension_semantics=sem, vmem_limit_bytes=VMEM_LIMIT)


def _proj_kernel(x_ref, g_ref, w_ref, gcol_ref, bd_ref, c_ref, s1_ref, s2_ref, o_ref, h_ref):
    j = pl.program_id(1)

    @pl.when(j == 0)
    def _():
        x = x_ref[...]
        ms = jnp.mean(x * x, axis=-1, keepdims=True)
        h_ref[...] = (x * lax.rsqrt(ms + NORM_EPS) * g_ref[...]).astype(BF16)

    def half_dot(k):
        return jnp.dot(h_ref[...], w_ref[:, k * PROJ_HALF:(k + 1) * PROJ_HALF], preferred_element_type=F32)

    def head_normed(a, col):
        ssq = jnp.dot((a * a).astype(BF16), bd_ref[...], preferred_element_type=F32)
        return a * lax.rsqrt(ssq * (1.0 / HEAD_DIM) + NORM_EPS) * gcol_ref[:, col:col + LANES]

    def finish(acc, k, kind):
        base = k * PROJ_HALF
        if kind == "plain":
            o_ref[:, base:base + PROJ_HALF] = acc.astype(o_ref.dtype)
            return
        for c in range(0, PROJ_HALF, LANES):
            y = head_normed(acc[:, c:c + LANES], base + c)
            if kind == "rope":
                up = pltpu.roll(y, LANES - ROT_DIM // 2, 1)
                dn = pltpu.roll(y, ROT_DIM // 2, 1)
                y = y * c_ref[...] + up * s1_ref[...] + dn * s2_ref[...]
            o_ref[:, base + c:base + c + LANES] = y.astype(o_ref.dtype)

    def tile(kinds):
        accs = [half_dot(0), half_dot(1)]
        finish(accs[0], 0, kinds[0])
        finish(accs[1], 1, kinds[1])

    for jt, kinds in enumerate(PROJ_TILE_KINDS):
        pl.when(j == jt)(functools.partial(tile, kinds))


def _rope_tables(positions):
    inv = ROPE_THETA ** (-np.arange(0, ROT_DIM, 2, dtype=np.float32) / ROT_DIM)
    ang = positions.astype(np.float32)[:, None] * inv[None, :]
    cos, sin = np.cos(ang), np.sin(ang)
    half = ROT_DIM // 2
    n = positions.shape[0]
    c = np.ones((n, HEAD_DIM), np.float32)
    s1 = np.zeros((n, HEAD_DIM), np.float32)
    s2 = np.zeros((n, HEAD_DIM), np.float32)
    c[:, :half] = cos
    c[:, half:ROT_DIM] = cos
    s1[:, :half] = -sin
    s2[:, half:ROT_DIM] = sin
    rep = LANES // HEAD_DIM
    return tuple(jnp.asarray(np.tile(t, (1, rep))) for t in (c, s1, s2))


def _proj(x2d, seq_len, positions, g_attn, w_cat, gcol, bd):
    n, d = x2d.shape
    tm = min(1024, seq_len)
    assert seq_len % tm == 0 and n % tm == 0
    assert PROJ_COLS == PROJ_TN * len(PROJ_TILE_KINDS)
    per_seq = seq_len // tm
    c, s1, s2 = _rope_tables(positions)
    grid = (n // tm, PROJ_COLS // PROJ_TN)
    tab = pl.BlockSpec((tm, LANES), lambda i, j: (i % per_seq, 0))
    return pl.pallas_call(
        _proj_kernel,
        grid=grid,
        in_specs=[
            pl.BlockSpec((tm, d), lambda i, j: (i, 0)),
            pl.BlockSpec((1, d), lambda i, j: (0, 0)),
            pl.BlockSpec((d, PROJ_TN), lambda i, j: (0, j)),
            pl.BlockSpec((1, PROJ_TN), lambda i, j: (0, j)),
            pl.BlockSpec((LANES, LANES), lambda i, j: (0, 0)),
            tab, tab, tab,
        ],
        out_specs=pl.BlockSpec((tm, PROJ_TN), lambda i, j: (i, j)),
        out_shape=jax.ShapeDtypeStruct((n, PROJ_COLS), BF16),
        scratch_shapes=[pltpu.VMEM((tm, d), BF16)],
        compiler_params=_cparams(("parallel", "arbitrary")),
        name="proj",
    )(x2d, g_attn, w_cat, gcol, bd, c, s1, s2)


ATTN_A_UNROLL = 8


def _attn_a_kernel(q_ref, k_ref, v_ref, km_ref, vm_ref, bias_ref, bmeta_ref, o_ref, sa_ref, sb_ref, ma_ref, mb_ref,
                   *, rows):
    lane = lax.broadcasted_iota(I32, (GRID_W, LANES), 1)
    first = lane < HEAD_DIM
    km = km_ref[...]
    vm = vm_ref[...]
    bmeta = bmeta_ref[0]
    nt = (((1,), (1,)), ((), ()))

    def band_start(r):
        return pl.multiple_of(jnp.clip(r - NA_KH // 2, 0, rows - NA_KH) * GRID_W, GRID_W)

    def scores(r, s_ref, m_ref):
        r = jnp.minimum(r, rows - 1)
        r0 = jnp.clip(r - NA_KH // 2, 0, rows - NA_KH)
        q = q_ref[pl.ds(pl.multiple_of(r * GRID_W, GRID_W), GRID_W), :]
        zero = jnp.zeros_like(q)
        qs = jnp.concatenate([jnp.where(first, q, zero), jnp.where(first, zero, q)], axis=0)
        kb = k_ref[pl.ds(band_start(r), NA_KH * GRID_W), :]
        s_ref[...] = lax.dot_general(qs, kb, nt, preferred_element_type=F32) + bias_ref[0, r - r0]
        m_ref[...] = lax.dot_general(qs, km, nt, preferred_element_type=F32) + bmeta

    def finish(r, s_ref, m_ref):
        s = s_ref[...]
        sm = m_ref[...]
        vb = v_ref[pl.ds(band_start(r), NA_KH * GRID_W), :]
        m = jnp.maximum(jnp.max(s, axis=-1, keepdims=True), jnp.max(sm, axis=-1, keepdims=True))
        p = jnp.exp(s - m)
        pm = jnp.exp(sm - m)
        l = jnp.sum(p, axis=-1, keepdims=True) + jnp.sum(pm, axis=-1, keepdims=True)
        o = (jnp.dot(p.astype(BF16), vb, preferred_element_type=F32)
             + jnp.dot(pm.astype(BF16), vm, preferred_element_type=F32)) / l
        o_ref[pl.ds(pl.multiple_of(r * GRID_W, GRID_W), GRID_W), :] = (
            jnp.where(first, o[:GRID_W], o[GRID_W:]).astype(o_ref.dtype))

    scores(0, sa_ref, ma_ref)

    def pair_body(h, carry):
        r = 2 * h
        scores(r + 1, sb_ref, mb_ref)
        finish(r, sa_ref, ma_ref)
        scores(r + 2, sa_ref, ma_ref)
        finish(r + 1, sb_ref, mb_ref)
        return carry

    lax.fori_loop(0, rows // 2, pair_body, 0, unroll=ATTN_A_UNROLL // 2)


def _attn_a(proj, meta_proj, bias_tab, bias_meta, batch, seq_len):
    rows = seq_len // GRID_W
    assert rows >= NA_KH and rows % ATTN_A_UNROLL == 0
    n = batch * seq_len
    pairs = WIDTH_A // LANES
    kernel = functools.partial(_attn_a_kernel, rows=rows)
    return pl.pallas_call(
        kernel,
        grid=(batch, pairs),
        in_specs=[
            pl.BlockSpec((seq_len, LANES), lambda b, p: (b, SEG_QA // LANES + p)),
            pl.BlockSpec((seq_len, LANES), lambda b, p: (b, SEG_KA // LANES + p)),
            pl.BlockSpec((seq_len, LANES), lambda b, p: (b, SEG_VA // LANES + p)),
            pl.BlockSpec((META_PAD, LANES), lambda b, p: (0, SEG_KA // LANES + p)),
            pl.BlockSpec((META_PAD, LANES), lambda b, p: (0, SEG_VA // LANES + p)),
            pl.BlockSpec((1, NA_KH, 2 * GRID_W, NA_KH * GRID_W), lambda b, p: (p, 0, 0, 0)),
            pl.BlockSpec((1, 2 * GRID_W, META_PAD), lambda b, p: (p, 0, 0)),
        ],
        out_specs=pl.BlockSpec((seq_len, LANES), lambda b, p: (b, p)),
        out_shape=jax.ShapeDtypeStruct((n, WIDTH_A), BF16),
        scratch_shapes=[pltpu.VMEM((2 * GRID_W, NA_KH * GRID_W), F32), pltpu.VMEM((2 * GRID_W, NA_KH * GRID_W), F32),
                        pltpu.VMEM((2 * GRID_W, META_PAD), F32), pltpu.VMEM((2 * GRID_W, META_PAD), F32)],
        compiler_params=_cparams(("parallel", "parallel")),
        name="attn_a",
    )(proj, proj, proj, meta_proj, meta_proj, bias_tab, bias_meta)


def _bias_tables_a(rpb, rpb_meta):
    n_dr, n_dc = 2 * NA_KH - 1, 2 * NA_KW - 1
    var = np.arange(NA_KH)[:, None]
    jj = np.arange(NA_KH)[None, :]
    ridx = jj - var + NA_KH - 1
    cq = np.arange(GRID_W)
    col_start = np.clip(cq - NA_KW // 2, 0, GRID_W - NA_KW)
    col_valid = (cq[None, :] >= col_start[:, None]) & (cq[None, :] < col_start[:, None] + NA_KW)
    cidx = np.clip(cq[None, :] - cq[:, None], -(NA_KW - 1), NA_KW - 1) + NA_KW - 1
    sel_r = (ridx[None, :, :] == np.arange(n_dr)[:, None, None]).astype(np.float32)
    sel_c = (cidx[None, :, :] == np.arange(n_dc)[:, None, None]).astype(np.float32)
    hi = lax.Precision.HIGHEST
    t1 = jnp.einsum('hab,bcw->hacw', rpb.astype(F32), jnp.asarray(sel_c), precision=hi)
    tab = jnp.einsum('avj,hacw->hvcjw', jnp.asarray(sel_r), t1, precision=hi)
    tab = jnp.where(col_valid[None, None, :, None, :], tab, NEG_INF)
    tab = tab.reshape(N_HEADS_A // 2, 2, NA_KH, GRID_W, NA_KH * GRID_W)
    tab = jnp.transpose(tab, (0, 2, 1, 3, 4)).reshape(N_HEADS_A // 2, NA_KH, 2 * GRID_W, NA_KH * GRID_W)
    bm = jnp.full((N_HEADS_A, META_PAD), NEG_INF, F32).at[:, :N_META].set(rpb_meta.astype(F32))
    bm = jnp.broadcast_to(bm[:, None, :], (N_HEADS_A, GRID_W, META_PAD)).reshape(N_HEADS_A // 2, 2 * GRID_W, META_PAD)
    return tab, bm


ATTN_B_UNROLL = 4


def _attn_b_kernel(sink_ref, q_ref, k_ref, v_ref, km_ref, vm_ref, bmeta_ref, o_ref, *, seq_len):
    kv = pl.program_id(1)
    nb = seq_len // WBLOCK
    span = 3 * WBLOCK
    group = N_HEADS_B // N_KV_B
    stack = group * WBLOCK
    lane = lax.broadcasted_iota(I32, (WBLOCK, LANES), 1)
    first = lane < HEAD_DIM
    km = km_ref[...]
    vm = vm_ref[...]
    bmeta = bmeta_ref[...]
    nt = (((1,), (1,)), ((), ()))
    qi = lax.broadcasted_iota(I32, (stack, span), 0) % WBLOCK
    kj = lax.broadcasted_iota(I32, (stack, span), 1)
    rel = kj - qi
    head_of_row = lax.broadcasted_iota(I32, (stack, 1), 0) // WBLOCK
    sink = jnp.zeros((stack, 1), F32)
    for g in range(group):
        sink = jnp.where(head_of_row == g, sink_ref[kv * group + g], sink)

    def blk_body(n, carry):
        start = jnp.clip((n - 1) * WBLOCK, 0, seq_len - span)
        start = pl.multiple_of(start, WBLOCK)
        q0 = pl.multiple_of(n * WBLOCK, WBLOCK)
        kb = k_ref[pl.ds(start, span), :]
        vb = v_ref[pl.ds(start, span), :]
        parts = []
        for c in range(group // 2):
            q = q_ref[pl.ds(q0, WBLOCK), c * LANES:(c + 1) * LANES]
            zero = jnp.zeros_like(q)
            parts += [jnp.where(first, q, zero), jnp.where(first, zero, q)]
        qs = jnp.concatenate(parts, axis=0)
        s = lax.dot_general(qs, kb, nt, preferred_element_type=F32)
        s = jnp.where(jnp.abs(rel + (start - q0)) <= WINDOW, s, NEG_INF)
        sm = lax.dot_general(qs, km, nt, preferred_element_type=F32) + bmeta
        m = jnp.maximum(jnp.max(s, axis=-1, keepdims=True), jnp.max(sm, axis=-1, keepdims=True))
        m = jnp.maximum(m, sink)
        p = jnp.exp(s - m)
        pm = jnp.exp(sm - m)
        l = jnp.sum(p, axis=-1, keepdims=True) + jnp.sum(pm, axis=-1, keepdims=True) + jnp.exp(sink - m)
        o = (jnp.dot(p.astype(BF16), vb, preferred_element_type=F32)
             + jnp.dot(pm.astype(BF16), vm, preferred_element_type=F32)) / l
        for c in range(group // 2):
            base = 2 * c * WBLOCK
            o_ref[pl.ds(q0, WBLOCK), c * LANES:(c + 1) * LANES] = jnp.where(
                first, o[base:base + WBLOCK], o[base + WBLOCK:base + 2 * WBLOCK]).astype(o_ref.dtype)
        return carry

    lax.fori_loop(0, nb, blk_body, 0, unroll=ATTN_B_UNROLL)


def _attn_b(proj, meta_proj, sinks, batch, seq_len):
    assert seq_len % (WBLOCK * ATTN_B_UNROLL) == 0 and seq_len >= 3 * WBLOCK
    n = batch * seq_len
    qw = WIDTH_B // N_KV_B
    bmeta = jnp.where(jnp.arange(META_PAD) < N_META, 0.0, NEG_INF).astype(F32)[None, :]
    kernel = functools.partial(_attn_b_kernel, seq_len=seq_len)
    grid_spec = pltpu.PrefetchScalarGridSpec(
        num_scalar_prefetch=1,
        grid=(batch, N_KV_B),
        in_specs=[
            pl.BlockSpec((seq_len, qw), lambda b, k, s: (b, SEG_QB // qw + k)),
            pl.BlockSpec((seq_len, LANES), lambda b, k, s: (b, SEG_KB // LANES + k)),
            pl.BlockSpec((seq_len, LANES), lambda b, k, s: (b, SEG_VB // LANES + k)),
            pl.BlockSpec((META_PAD, LANES), lambda b, k, s: (0, SEG_KB // LANES + k)),
            pl.BlockSpec((META_PAD, LANES), lambda b, k, s: (0, SEG_VB // LANES + k)),
            pl.BlockSpec((1, META_PAD), lambda b, k, s: (0, 0)),
        ],
        out_specs=pl.BlockSpec((seq_len, qw), lambda b, k, s: (b, k)),
    )
    return pl.pallas_call(
        kernel,
        grid_spec=grid_spec,
        out_shape=jax.ShapeDtypeStruct((n, WIDTH_B), BF16),
        compiler_params=_cparams(("parallel", "parallel")),
        name="attn_b",
    )(sinks.astype(F32), proj, proj, proj, meta_proj, meta_proj, bmeta)


def _pack_halves(a):
    w = a.shape[1] // 2
    lo = pltpu.bitcast(a[:, :w].astype(BF16).astype(F32), U32)
    hi = pltpu.bitcast(a[:, w:].astype(BF16).astype(F32), U32)
    return (lo >> 16) | (hi & jnp.uint32(0xFFFF0000))


def _unpack_halves(u):
    lo = pltpu.bitcast(u << 16, F32)
    hi = pltpu.bitcast(u & jnp.uint32(0xFFFF0000), F32)
    return lo, hi


def _post_kernel(oa_ref, ob_ref, x_ref, ga_ref, gb_ref, wo_ref, gm_ref, wrh_ref, wrl_ref, br_ref,
                 tri_ref, cnt0_ref, x1_ref, xn_ref, ei_ref, gate_ref, cnt_ref, run_ref):
    i = pl.program_id(0)

    @pl.when(i == 0)
    def _():
        run_ref[...] = cnt0_ref[...]

    def normed(ref, g_ref):
        a = ref[...].astype(F32)
        ms = jnp.mean(a * a, axis=-1, keepdims=True)
        return (a * lax.rsqrt(ms + NORM_EPS) * g_ref[...]).astype(BF16)

    wa = oa_ref.shape[1]
    mix = (jnp.dot(normed(oa_ref, ga_ref), wo_ref[:wa, :], preferred_element_type=F32)
           + jnp.dot(normed(ob_ref, gb_ref), wo_ref[wa:, :], preferred_element_type=F32))
    x1 = x_ref[...] + mix
    x1_ref[...] = x1
    ms = jnp.mean(x1 * x1, axis=-1, keepdims=True)
    xn = x1 * lax.rsqrt(ms + NORM_EPS) * gm_ref[...]
    xn_ref[...] = _pack_halves(xn)

    xh = xn.astype(BF16)
    xl = (xn - xh.astype(F32)).astype(BF16)
    logits = (jnp.dot(xh, wrh_ref[...], preferred_element_type=F32)
              + jnp.dot(xl, wrh_ref[...], preferred_element_type=F32)
              + jnp.dot(xh, wrl_ref[...], preferred_element_type=F32)) + br_ref[...]

    tm = logits.shape[0]
    lane = lax.broadcasted_iota(I32, (tm, LANES), 1)
    lanef = lane.astype(F32)
    work = logits
    vals, idxs = [], []
    chosen = jnp.zeros((tm, LANES), F32)
    for _ in range(TOP_K):
        mk = jnp.max(work, axis=-1, keepdims=True)
        ik = jnp.min(jnp.where(work == mk, lanef, float(LANES)), axis=-1, keepdims=True).astype(I32)
        hit = lane == ik
        work = jnp.where(hit, -jnp.inf, work)
        chosen = jnp.where(hit, 1.0, chosen)
        vals.append(mk)
        idxs.append(ik)
    ex = [jnp.exp(v - vals[0]) for v in vals]
    den = ex[0] + ex[1] + ex[2] + ex[3]

    prefix = jnp.dot(tri_ref[...], chosen.astype(BF16), preferred_element_type=F32) + run_ref[...]
    run_ref[...] = run_ref[...] + jnp.sum(chosen, axis=0, keepdims=True)
    cnt_ref[...] = run_ref[...]

    ei = jnp.zeros((tm, LANES), I32)
    gates = jnp.zeros((tm, LANES), F32)
    for k in range(TOP_K):
        rank = jnp.sum(jnp.where(lane == idxs[k], prefix, 0.0), axis=-1, keepdims=True).astype(I32)
        ei = jnp.where(lane == k, idxs[k], ei)
        ei = jnp.where(lane == TOP_K + k, rank, ei)
        gates = jnp.where(lane == k, ex[k] / den, gates)
    ei_ref[...] = ei
    gate_ref[...] = gates


def _post(oa, ob, x2d, ga, gb, wo, gm, wrh, wrl, br, cnt0):
    n, d = x2d.shape
    tm = 256
    assert n % tm == 0
    tri = jnp.asarray(np.tril(np.ones((tm, tm), np.float32), -1), BF16)
    row = lambda i: (i, 0)
    fixed = lambda i: (0, 0)
    return pl.pallas_call(
        _post_kernel,
        grid=(n // tm,),
        in_specs=[
            pl.BlockSpec((tm, oa.shape[1]), row),
            pl.BlockSpec((tm, ob.shape[1]), row),
            pl.BlockSpec((tm, d), row),
            pl.BlockSpec((1, oa.shape[1]), fixed),
            pl.BlockSpec((1, ob.shape[1]), fixed),
            pl.BlockSpec(wo.shape, fixed),
            pl.BlockSpec((1, d), fixed),
            pl.BlockSpec((d, LANES), fixed),
            pl.BlockSpec((d, LANES), fixed),
            pl.BlockSpec((1, LANES), fixed),
            pl.BlockSpec((tm, tm), fixed),
            pl.BlockSpec((1, LANES), fixed),
        ],
        out_specs=[
            pl.BlockSpec((tm, d), row),
            pl.BlockSpec((tm, d // 2), row),
            pl.BlockSpec((tm, LANES), row),
            pl.BlockSpec((tm, LANES), row),
            pl.BlockSpec((1, LANES), fixed),
        ],
        out_shape=[
            jax.ShapeDtypeStruct((n, d), F32),
            jax.ShapeDtypeStruct((n, d // 2), U32),
            jax.ShapeDtypeStruct((n, LANES), I32),
            jax.ShapeDtypeStruct((n, LANES), F32),
            jax.ShapeDtypeStruct((1, LANES), F32),
        ],
        scratch_shapes=[pltpu.VMEM((1, LANES), F32)],
        compiler_params=_cparams(("arbitrary",)),
        name="post",
    )(oa, ob, x2d, ga, gb, wo, gm, wrh, wrl, br, tri, cnt0)


ROW_DMA_UNROLL = 4


def _dispatch_kernel(slot_ref, xn_ref, xs_in_ref, xs_ref, sems, *, tk):
    del xs_in_ref

    def row_copy(t, k):
        dst = slot_ref[0, 0, t * TOP_K + k]
        return pltpu.make_async_copy(xn_ref.at[pl.ds(t, 1)], xs_ref.at[pl.ds(dst, 1)], sems.at[k])

    def issue(t, carry):
        for k in range(TOP_K):
            row_copy(t, k).start(priority=k % 2)
        return carry

    lax.fori_loop(0, tk, issue, 0, unroll=ROW_DMA_UNROLL)
    for k in range(TOP_K):
        pltpu.make_async_copy(xn_ref, xs_ref.at[pl.ds(0, tk)], sems.at[k]).wait()


def _dispatch(slots, xn, xs):
    n, w = xn.shape
    tk = 512
    assert n % tk == 0
    slots3 = slots.reshape(n // tk, 1, tk * TOP_K)
    kernel = functools.partial(_dispatch_kernel, tk=tk)
    return pl.pallas_call(
        kernel,
        grid=(n // tk,),
        in_specs=[
            pl.BlockSpec((1, 1, tk * TOP_K), lambda i: (i, 0, 0), memory_space=pltpu.SMEM),
            pl.BlockSpec((tk, w), lambda i: (i, 0)),
            pl.BlockSpec(memory_space=pl.ANY),
        ],
        out_specs=pl.BlockSpec(memory_space=pl.ANY),
        out_shape=jax.ShapeDtypeStruct(xs.shape, xs.dtype),
        input_output_aliases={2: 0},
        scratch_shapes=[pltpu.SemaphoreType.DMA((TOP_K,))],
        compiler_params=_cparams(("arbitrary",)),
        name="dispatch",
    )(slots3, xn, xs)


MOE_TM = 512
MOE_TF = 1024
FP8 = jnp.float8_e4m3fn
FP8_TOP = 224.0
ACT_SCALE = 4.0


def _pow2_scale(amax):
    safe = jnp.where(amax > 0, amax, FP8_TOP)
    return jnp.exp2(jnp.floor(jnp.log2(FP8_TOP / safe)))


def _quantize_kernel(w_ref, q_ref, s_ref):
    w = w_ref[0]
    scale = _pow2_scale(jnp.max(jnp.abs(w), axis=(0, 1), keepdims=True))
    q_ref[0] = (w * scale).astype(FP8)
    s_ref[...] = jnp.broadcast_to(1.0 / scale, s_ref.shape)


def _fp8_expert_weights(w, block):
    n_e, a, b = w.shape
    ta, tb = block
    assert a % ta == 0 and b % tb == 0 and (a == ta or b == tb)
    tiles = (a // ta) * (b // tb)
    tile_index = (lambda e, t: (e, t, 0)) if b == tb else (lambda e, t: (e, 0, t))
    q, s = pl.pallas_call(
        _quantize_kernel,
        grid=(n_e, tiles),
        in_specs=[pl.BlockSpec((1, ta, tb), tile_index)],
        out_specs=[pl.BlockSpec((1, ta, tb), tile_index),
                   pl.BlockSpec((1, 1, 8, LANES), lambda e, t: (e, t, 0, 0))],
        out_shape=[jax.ShapeDtypeStruct(w.shape, FP8), jax.ShapeDtypeStruct((n_e, tiles, 8, LANES), F32)],
        compiler_params=_cparams(("parallel", "parallel")),
        name="quantize",
    )(w)
    return q, s[:, :, 0, 0].reshape(-1)


def _experts_kernel(bexp_ref, bval_ref, sg_ref, su_ref, sd_ref, xs_ref, wg_ref, bg_ref, wu_ref, bu_ref,
                    wd_ref, bd_ref, ys_ref, lo_ref, hi_ref, rinv_ref, acc_ref):
    i = pl.program_id(0)
    j = pl.program_id(1)
    nvalid = bval_ref[i]
    tile = bexp_ref[i] * pl.num_programs(1) + j

    @pl.when(jnp.logical_and(nvalid == 0, j == 0))
    def _():
        ys_ref[...] = jnp.zeros_like(ys_ref)

    @pl.when(nvalid > 0)
    def _():
        @pl.when(j == 0)
        def _():
            u = xs_ref[...]
            rows = lax.broadcasted_iota(I32, u.shape, 0)
            u = jnp.where(rows < nvalid, u, jnp.zeros_like(u))
            lo, hi = _unpack_halves(u)
            amax = jnp.maximum(jnp.max(jnp.abs(lo), axis=-1, keepdims=True),
                               jnp.max(jnp.abs(hi), axis=-1, keepdims=True))
            rs = _pow2_scale(amax)
            lo_ref[...] = (lo * rs).astype(FP8)
            hi_ref[...] = (hi * rs).astype(FP8)
            rinv_ref[...] = jnp.broadcast_to(1.0 / rs, rinv_ref.shape)
            acc_ref[...] = jnp.zeros_like(acc_ref)

        half = lo_ref.shape[1]
        lo = lo_ref[...]
        hi = hi_ref[...]
        rinv = rinv_ref[:, :1]
        hg = (jnp.dot(lo, wg_ref[0, :half, :], preferred_element_type=F32)
              + jnp.dot(hi, wg_ref[0, half:, :], preferred_element_type=F32)) * (rinv * sg_ref[tile]) + bg_ref[0]
        hu = (jnp.dot(lo, wu_ref[0, :half, :], preferred_element_type=F32)
              + jnp.dot(hi, wu_ref[0, half:, :], preferred_element_type=F32)) * (rinv * su_ref[tile]) + bu_ref[0]
        g = jnp.minimum(hg, SWIGLU_LIMIT)
        u = jnp.clip(hu, -SWIGLU_LIMIT, SWIGLU_LIMIT)
        act = (u + 1.0) * (g * jax.nn.sigmoid(SWIGLU_ALPHA * g))
        down = jnp.dot((act * ACT_SCALE).astype(FP8), wd_ref[0], preferred_element_type=F32)
        acc_ref[...] += down * (sd_ref[tile] * (1.0 / ACT_SCALE))

        @pl.when(j == pl.num_programs(1) - 1)
        def _():
            ys_ref[...] = _pack_halves(acc_ref[...] + bd_ref[0])


def _experts(blk_exp, blk_valid, xs, wg, sg, bg, wu, su, bu, wd, sd, bd, tm, tf):
    p, half = xs.shape
    d = 2 * half
    dff = wg.shape[2]
    assert p % tm == 0 and dff % tf == 0
    nblk = p // tm
    nf = dff // tf

    def jeff(i, j, bval):
        return jnp.where(bval[i] > 0, j, nf - 1)

    def wspec(shape, imap):
        return pl.BlockSpec(shape, lambda i, j, be, bv, *_: imap(be[i], jeff(i, j, bv)))

    grid_spec = pltpu.PrefetchScalarGridSpec(
        num_scalar_prefetch=5,
        grid=(nblk, nf),
        in_specs=[
            pl.BlockSpec((tm, half), lambda i, j, *_: (i, 0)),
            wspec((1, d, tf), lambda e, f: (e, 0, f)),
            wspec((1, 1, tf), lambda e, f: (e, 0, f)),
            wspec((1, d, tf), lambda e, f: (e, 0, f)),
            wspec((1, 1, tf), lambda e, f: (e, 0, f)),
            wspec((1, tf, d), lambda e, f: (e, f, 0)),
            wspec((1, 1, d), lambda e, f: (e, 0, 0)),
        ],
        out_specs=pl.BlockSpec((tm, half), lambda i, j, *_: (i, 0)),
        scratch_shapes=[pltpu.VMEM((tm, half), FP8), pltpu.VMEM((tm, half), FP8),
                        pltpu.VMEM((tm, LANES), F32), pltpu.VMEM((tm, d), F32)],
    )
    return pl.pallas_call(
        _experts_kernel,
        grid_spec=grid_spec,
        out_shape=jax.ShapeDtypeStruct((p, half), U32),
        compiler_params=_cparams(("arbitrary", "arbitrary")),
        name="experts",
    )(blk_exp, blk_valid, sg, su, sd, xs, wg, bg, wu, bu, wd, bd)


def _combine_kernel(slot_ref, slot_next_ref, x1_ref, gate_ref, ys_ref, o_ref, buf_ref, sems, *, tk):
    i = pl.program_id(0)
    cur = i % 2

    def gather(slots, b):
        def issue(t, carry):
            for k in range(TOP_K):
                src = slots[0, 0, t * TOP_K + k]
                pltpu.make_async_copy(ys_ref.at[pl.ds(src, 1)], buf_ref.at[b, k, pl.ds(t, 1)],
                                      sems.at[b, k]).start(priority=k % 2)
            return carry
        lax.fori_loop(0, tk, issue, 0, unroll=ROW_DMA_UNROLL)

    @pl.when(i == 0)
    def _():
        gather(slot_ref, 0)

    @pl.when(i + 1 < pl.num_programs(0))
    def _():
        gather(slot_next_ref, 1 - cur)

    half = buf_ref.shape[3]
    gates = gate_ref[...]
    acc_lo = x1_ref[:, :half]
    acc_hi = x1_ref[:, half:]
    for k in range(TOP_K):
        pltpu.make_async_copy(ys_ref.at[pl.ds(0, tk)], buf_ref.at[cur, k], sems.at[cur, k]).wait()
        lo, hi = _unpack_halves(buf_ref[cur, k])
        g = gates[:, k:k + 1]
        acc_lo = acc_lo + g * lo
        acc_hi = acc_hi + g * hi
    o_ref[:, :half] = acc_lo
    o_ref[:, half:] = acc_hi


def _combine(slots, x1, gates, ys):
    n, d = x1.shape
    tk = 256
    assert n % tk == 0
    steps = n // tk
    slots3 = slots.reshape(steps, 1, tk * TOP_K)
    kernel = functools.partial(_combine_kernel, tk=tk)
    return pl.pallas_call(
        kernel,
        grid=(steps,),
        in_specs=[
            pl.BlockSpec((1, 1, tk * TOP_K), lambda i: (i, 0, 0), memory_space=pltpu.SMEM),
            pl.BlockSpec((1, 1, tk * TOP_K), lambda i: (jnp.minimum(i + 1, steps - 1), 0, 0),
                         memory_space=pltpu.SMEM),
            pl.BlockSpec((tk, d), lambda i: (i, 0)),
            pl.BlockSpec((tk, LANES), lambda i: (i, 0)),
            pl.BlockSpec(memory_space=pl.ANY),
        ],
        out_specs=pl.BlockSpec((tk, d), lambda i: (i, 0)),
        out_shape=jax.ShapeDtypeStruct((n, d), F32),
        scratch_shapes=[pltpu.VMEM((2, TOP_K, tk, d // 2), U32), pltpu.SemaphoreType.DMA((2, TOP_K))],
        compiler_params=_cparams(("arbitrary",)),
        name="combine",
    )(slots3, slots3, x1, gates, ys)


def _dup_kv_heads(w):
    d = w.shape[0]
    w = w.reshape(d, N_KV_B, 1, HEAD_DIM)
    return jnp.broadcast_to(w, (d, N_KV_B, 2, HEAD_DIM)).reshape(d, 2 * KV_WIDTH_B)


def kernel(x_prompt, x_sample, meta_tokens, norm_attn, w_in, q_norm_a, k_norm_a, rpb, rpb_meta, q_norm_b, k_norm_b, sinks, out_norm_a, out_norm_b, w_out, norm_mlp, w_router, b_router, w_gate, b_gate, w_up, b_up, w_down, b_down):
    depth = norm_attn.shape[0]
    assert depth == 1
    d = x_prompt.shape[-1]
    groups = [(x_prompt.reshape(-1, d), x_prompt.shape[0], x_prompt.shape[1]),
              (x_sample.reshape(-1, d), x_sample.shape[0], x_sample.shape[1])]

    wi = w_in[0]
    w_cat = jnp.concatenate([wi[:, :SEG_KB], _dup_kv_heads(wi[:, SEG_KB:SEG_KB + KV_WIDTH_B]),
                             _dup_kv_heads(wi[:, SEG_KB + KV_WIDTH_B:])], axis=1).astype(BF16)
    scale = HEAD_DIM ** -0.5
    gcol = jnp.concatenate([
        jnp.tile(q_norm_a[0].astype(F32) * scale, N_HEADS_A), jnp.tile(k_norm_a[0].astype(F32), N_HEADS_A),
        jnp.ones((WIDTH_A,), F32),
        jnp.tile(q_norm_b[0].astype(F32) * scale, N_HEADS_B), jnp.tile(k_norm_b[0].astype(F32), 2 * N_KV_B),
        jnp.ones((2 * KV_WIDTH_B,), F32)])[None, :]
    bd = jnp.asarray(np.kron(np.eye(LANES // HEAD_DIM, dtype=np.float32),
                             np.ones((HEAD_DIM, HEAD_DIM), np.float32)), BF16)
    g_attn = norm_attn[0].astype(F32)[None, :]
    bias_tab, bias_meta = _bias_tables_a(rpb[0], rpb_meta[0])
    wo = w_out[0].astype(BF16)
    ga = out_norm_a[0].astype(F32)[None, :]
    gb = out_norm_b[0].astype(F32)[None, :]
    gm = norm_mlp[0].astype(F32)[None, :]
    wr = jnp.zeros((d, LANES), F32).at[:, :N_EXPERTS].set(w_router[0].astype(F32))
    wrh = wr.astype(BF16)
    wrl = (wr - wrh.astype(F32)).astype(BF16)
    br = jnp.full((1, LANES), NEG_INF, F32).at[0, :N_EXPERTS].set(b_router[0].astype(F32))
    wg, sg = _fp8_expert_weights(w_gate[0].astype(F32), (d, MOE_TF))
    wu, su = _fp8_expert_weights(w_up[0].astype(F32), (d, MOE_TF))
    wd, sd = _fp8_expert_weights(w_down[0].astype(F32), (MOE_TF, d))
    bg = b_gate[0].astype(F32)[:, None, :]
    bu = b_up[0].astype(F32)[:, None, :]
    bdn = b_down[0].astype(F32)[:, None, :]

    meta_x = jnp.zeros((META_PAD, d), F32).at[:N_META].set(meta_tokens.astype(F32))
    meta_pos = np.minimum(np.arange(META_PAD), N_META - 1)
    meta_proj = _proj(meta_x, META_PAD, meta_pos, g_attn, w_cat, gcol, bd)
    meta_proj = jnp.where(jnp.arange(META_PAD)[:, None] < N_META, meta_proj, jnp.zeros_like(meta_proj))

    cnt = jnp.zeros((1, LANES), F32)
    staged = []
    for x2d, batch, seq_len in groups:
        pos = N_META + np.arange(seq_len)
        proj = _proj(x2d, seq_len, pos, g_attn, w_cat, gcol, bd)
        oa = _attn_a(proj, meta_proj, bias_tab, bias_meta, batch, seq_len)
        ob = _attn_b(proj, meta_proj, sinks[0], batch, seq_len)
        x1, xn, ei, gates, cnt = _post(oa, ob, x2d, ga, gb, wo, gm, wrh, wrl, br, cnt)
        staged.append((x1, xn, ei, gates))

    total = sum(g[0].shape[0] for g in groups) * TOP_K
    nblk = (total + N_EXPERTS * (MOE_TM - 1) + MOE_TM - 1) // MOE_TM
    counts = cnt[0, :N_EXPERTS].astype(I32)
    padded = (counts + MOE_TM - 1) // MOE_TM * MOE_TM
    pends = jnp.cumsum(padded)
    pstarts = pends - padded
    blk_start = jnp.arange(nblk, dtype=I32) * MOE_TM
    blk_exp = jnp.minimum(jnp.sum((blk_start[:, None] >= pends[None, :]).astype(I32), axis=1), N_EXPERTS - 1)
    blk_valid = jnp.clip(counts[blk_exp] - (blk_start - pstarts[blk_exp]), 0, MOE_TM)
    blk_valid = jnp.where(blk_start < pends[-1], blk_valid, 0).astype(I32)
    last_used = jnp.maximum(pends[-1] // MOE_TM - 1, 0)
    blk_exp = jnp.where(blk_start < pends[-1], blk_exp, blk_exp[last_used]).astype(I32)

    slots = [pstarts[ei[:, :TOP_K]] + ei[:, TOP_K:2 * TOP_K] for (_, _, ei, _) in staged]

    xs = jnp.zeros((nblk * MOE_TM, d // 2), U32)
    for (x1, xn, ei, gates), sl in zip(staged, slots):
        xs = _dispatch(sl, xn, xs)
    ys = _experts(blk_exp, blk_valid, xs, wg, sg, bg, wu, su, bu, wd, sd, bdn, MOE_TM, MOE_TF)

    outs = []
    for (x1, xn, ei, gates), sl, (x2d, batch, seq_len) in zip(staged, slots, groups):
        y = _combine(sl, x1, gates, ys)
        outs.append(y.reshape(batch, seq_len, d))
    return tuple(outs)
```

```python
import functools

import jax
import jax.numpy as jnp
import numpy as np
from jax import lax
from jax.experimental import pallas as pl
from jax.experimental.pallas import tpu as pltpu

F32 = jnp.float32
BF16 = jnp.bfloat16
U32 = jnp.uint32
I32 = jnp.int32

HEAD_DIM = 64
N_HEADS_A = 16
N_HEADS_B = 16
N_KV_B = 4
WIDTH_A = N_HEADS_A * HEAD_DIM
WIDTH_B = N_HEADS_B * HEAD_DIM
KV_WIDTH_B = N_KV_B * HEAD_DIM
GRID_W = 64
NA_KH = 8
NA_KW = 16
WINDOW = 128
WBLOCK = 128
ROT_DIM = HEAD_DIM // 4
ROPE_THETA = 500000.0
N_META = 16
N_EXPERTS = 32
TOP_K = 4
SWIGLU_LIMIT = 7.0
SWIGLU_ALPHA = 1.702
NORM_EPS = 1e-5
NEG_INF = -1e30

LANES = 128
META_PAD = LANES
VMEM_LIMIT = 56 * 1024 * 1024

PROJ_TN = 1024
PROJ_HALF = PROJ_TN // 2
PROJ_TILE_KINDS = (("norm", "norm"), ("norm", "norm"), ("plain", "plain"), ("rope", "rope"), ("rope", "plain"))
SEG_QA, SEG_KA, SEG_VA, SEG_QB = 0, WIDTH_A, 2 * WIDTH_A, 3 * WIDTH_A
SEG_KB = 3 * WIDTH_A + WIDTH_B
SEG_VB = SEG_KB + 2 * KV_WIDTH_B
PROJ_COLS = SEG_VB + 2 * KV_WIDTH_B


def _cparams(sem):
    return pltpu.CompilerParams(dimension_semantics=sem, vmem_limit_bytes=VMEM_LIMIT)


def _proj_kernel(x_ref, g_ref, w_ref, gcol_ref, bd_ref, c_ref, s1_ref, s2_ref, o_ref, h_ref):
    j = pl.program_id(1)

    @pl.when(j == 0)
    def _():
        x = x_ref[...]
        ms = jnp.mean(x * x, axis=-1, keepdims=True)
        h_ref[...] = (x * lax.rsqrt(ms + NORM_EPS) * g_ref[...]).astype(BF16)

    def half_dot(k):
        return jnp.dot(h_ref[...], w_ref[:, k * PROJ_HALF:(k + 1) * PROJ_HALF], preferred_element_type=F32)

    def head_normed(a, col):
        ssq = jnp.dot((a * a).astype(BF16), bd_ref[...], preferred_element_type=F32)
        return a * lax.rsqrt(ssq * (1.0 / HEAD_DIM) + NORM_EPS) * gcol_ref[:, col:col + LANES]

    def finish(acc, k, kind):
        base = k * PROJ_HALF
        if kind == "plain":
            o_ref[:, base:base + PROJ_HALF] = acc.astype(o_ref.dtype)
            return
        for c in range(0, PROJ_HALF, LANES):
            y = head_normed(acc[:, c:c + LANES], base + c)
            if kind == "rope":
                up = pltpu.roll(y, LANES - ROT_DIM // 2, 1)
                dn = pltpu.roll(y, ROT_DIM // 2, 1)
                y = y * c_ref[...] + up * s1_ref[...] + dn * s2_ref[...]
            o_ref[:, base + c:base + c + LANES] = y.astype(o_ref.dtype)

    def tile(kinds):
        accs = [half_dot(0), half_dot(1)]
        finish(accs[0], 0, kinds[0])
        finish(accs[1], 1, kinds[1])

    for jt, kinds in enumerate(PROJ_TILE_KINDS):
        pl.when(j == jt)(functools.partial(tile, kinds))


def _rope_tables(positions):
    inv = ROPE_THETA ** (-np.arange(0, ROT_DIM, 2, dtype=np.float32) / ROT_DIM)
    ang = positions.astype(np.float32)[:, None] * inv[None, :]
    cos, sin = np.cos(ang), np.sin(ang)
    half = ROT_DIM // 2
    n = positions.shape[0]
    c = np.ones((n, HEAD_DIM), np.float32)
    s1 = np.zeros((n, HEAD_DIM), np.float32)
    s2 = np.zeros((n, HEAD_DIM), np.float32)
    c[:, :half] = cos
    c[:, half:ROT_DIM] = cos
    s1[:, :half] = -sin
    s2[:, half:ROT_DIM] = sin
    rep = LANES // HEAD_DIM
    return tuple(jnp.asarray(np.tile(t, (1, rep))) for t in (c, s1, s2))


def _proj(x2d, seq_len, positions, g_attn, w_cat, gcol, bd):
    n, d = x2d.shape
    tm = min(1024, seq_len)
    assert seq_len % tm == 0 and n % tm == 0
    assert PROJ_COLS == PROJ_TN * len(PROJ_TILE_KINDS)
    per_seq = seq_len // tm
    c, s1, s2 = _rope_tables(positions)
    grid = (n // tm, PROJ_COLS // PROJ_TN)
    tab = pl.BlockSpec((tm, LANES), lambda i, j: (i % per_seq, 0))
    return pl.pallas_call(
        _proj_kernel,
        grid=grid,
        in_specs=[
            pl.BlockSpec((tm, d), lambda i, j: (i, 0)),
            pl.BlockSpec((1, d), lambda i, j: (0, 0)),
            pl.BlockSpec((d, PROJ_TN), lambda i, j: (0, j)),
            pl.BlockSpec((1, PROJ_TN), lambda i, j: (0, j)),
            pl.BlockSpec((LANES, LANES), lambda i, j: (0, 0)),
            tab, tab, tab,
        ],
        out_specs=pl.BlockSpec((tm, PROJ_TN), lambda i, j: (i, j)),
        out_shape=jax.ShapeDtypeStruct((n, PROJ_COLS), BF16),
        scratch_shapes=[pltpu.VMEM((tm, d), BF16)],
        compiler_params=_cparams(("parallel", "arbitrary")),
        name="proj",
    )(x2d, g_attn, w_cat, gcol, bd, c, s1, s2)


ATTN_A_UNROLL = 16


def _attn_a_kernel(q_ref, k_ref, v_ref, km_ref, vm_ref, bias_ref, bmeta_ref, o_ref, sa_ref, sb_ref, ma_ref, mb_ref,
                   *, rows):
    lane = lax.broadcasted_iota(I32, (GRID_W, LANES), 1)
    first = lane < HEAD_DIM
    km = km_ref[...]
    vm = vm_ref[...]
    bmeta = bmeta_ref[0]
    nt = (((1,), (1,)), ((), ()))

    def band_start(r):
        return pl.multiple_of(jnp.clip(r - NA_KH // 2, 0, rows - NA_KH) * GRID_W, GRID_W)

    def scores(r, s_ref, m_ref):
        r = jnp.minimum(r, rows - 1)
        r0 = jnp.clip(r - NA_KH // 2, 0, rows - NA_KH)
        q = q_ref[pl.ds(pl.multiple_of(r * GRID_W, GRID_W), GRID_W), :]
        zero = jnp.zeros_like(q)
        qs = jnp.concatenate([jnp.where(first, q, zero), jnp.where(first, zero, q)], axis=0)
        kb = k_ref[pl.ds(band_start(r), NA_KH * GRID_W), :]
        s_ref[...] = lax.dot_general(qs, kb, nt, preferred_element_type=F32) + bias_ref[0, r - r0]
        m_ref[...] = lax.dot_general(qs, km, nt, preferred_element_type=F32) + bmeta

    def finish(r, s_ref, m_ref):
        s = s_ref[...]
        sm = m_ref[...]
        vb = v_ref[pl.ds(band_start(r), NA_KH * GRID_W), :]
        m = jnp.maximum(jnp.max(s, axis=-1, keepdims=True), jnp.max(sm, axis=-1, keepdims=True))
        p = jnp.exp(s - m)
        pm = jnp.exp(sm - m)
        l = jnp.sum(p, axis=-1, keepdims=True) + jnp.sum(pm, axis=-1, keepdims=True)
        o = (jnp.dot(p.astype(BF16), vb, preferred_element_type=F32)
             + jnp.dot(pm.astype(BF16), vm, preferred_element_type=F32)) / l
        o_ref[pl.ds(pl.multiple_of(r * GRID_W, GRID_W), GRID_W), :] = (
            jnp.where(first, o[:GRID_W], o[GRID_W:]).astype(o_ref.dtype))

    scores(0, sa_ref, ma_ref)

    def pair_body(h, carry):
        r = 2 * h
        scores(r + 1, sb_ref, mb_ref)
        finish(r, sa_ref, ma_ref)
        scores(r + 2, sa_ref, ma_ref)
        finish(r + 1, sb_ref, mb_ref)
        return carry

    lax.fori_loop(0, rows // 2, pair_body, 0, unroll=ATTN_A_UNROLL // 2)


def _attn_a(proj, meta_proj, bias_tab, bias_meta, batch, seq_len):
    rows = seq_len // GRID_W
    assert rows >= NA_KH and rows % ATTN_A_UNROLL == 0
    n = batch * seq_len
    pairs = WIDTH_A // LANES
    kernel = functools.partial(_attn_a_kernel, rows=rows)
    return pl.pallas_call(
        kernel,
        grid=(batch, pairs),
        in_specs=[
            pl.BlockSpec((seq_len, LANES), lambda b, p: (b, SEG_QA // LANES + p)),
            pl.BlockSpec((seq_len, LANES), lambda b, p: (b, SEG_KA // LANES + p)),
            pl.BlockSpec((seq_len, LANES), lambda b, p: (b, SEG_VA // LANES + p)),
            pl.BlockSpec((META_PAD, LANES), lambda b, p: (0, SEG_KA // LANES + p)),
            pl.BlockSpec((META_PAD, LANES), lambda b, p: (0, SEG_VA // LANES + p)),
            pl.BlockSpec((1, NA_KH, 2 * GRID_W, NA_KH * GRID_W), lambda b, p: (p, 0, 0, 0)),
            pl.BlockSpec((1, 2 * GRID_W, META_PAD), lambda b, p: (p, 0, 0)),
        ],
        out_specs=pl.BlockSpec((seq_len, LANES), lambda b, p: (b, p)),
        out_shape=jax.ShapeDtypeStruct((n, WIDTH_A), BF16),
        scratch_shapes=[pltpu.VMEM((2 * GRID_W, NA_KH * GRID_W), F32), pltpu.VMEM((2 * GRID_W, NA_KH * GRID_W), F32),
                        pltpu.VMEM((2 * GRID_W, META_PAD), F32), pltpu.VMEM((2 * GRID_W, META_PAD), F32)],
        compiler_params=_cparams(("parallel", "parallel")),
        name="attn_a",
    )(proj, proj, proj, meta_proj, meta_proj, bias_tab, bias_meta)


def _bias_tables_a(rpb, rpb_meta):
    n_dr, n_dc = 2 * NA_KH - 1, 2 * NA_KW - 1
    var = np.arange(NA_KH)[:, None]
    jj = np.arange(NA_KH)[None, :]
    ridx = jj - var + NA_KH - 1
    cq = np.arange(GRID_W)
    col_start = np.clip(cq - NA_KW // 2, 0, GRID_W - NA_KW)
    col_valid = (cq[None, :] >= col_start[:, None]) & (cq[None, :] < col_start[:, None] + NA_KW)
    cidx = np.clip(cq[None, :] - cq[:, None], -(NA_KW - 1), NA_KW - 1) + NA_KW - 1
    sel_r = (ridx[None, :, :] == np.arange(n_dr)[:, None, None]).astype(np.float32)
    sel_c = (cidx[None, :, :] == np.arange(n_dc)[:, None, None]).astype(np.float32)
    hi = lax.Precision.HIGHEST
    t1 = jnp.einsum('hab,bcw->hacw', rpb.astype(F32), jnp.asarray(sel_c), precision=hi)
    tab = jnp.einsum('avj,hacw->hvcjw', jnp.asarray(sel_r), t1, precision=hi)
    tab = jnp.where(col_valid[None, None, :, None, :], tab, NEG_INF)
    tab = tab.reshape(N_HEADS_A // 2, 2, NA_KH, GRID_W, NA_KH * GRID_W)
    tab = jnp.transpose(tab, (0, 2, 1, 3, 4)).reshape(N_HEADS_A // 2, NA_KH, 2 * GRID_W, NA_KH * GRID_W)
    bm = jnp.full((N_HEADS_A, META_PAD), NEG_INF, F32).at[:, :N_META].set(rpb_meta.astype(F32))
    bm = jnp.broadcast_to(bm[:, None, :], (N_HEADS_A, GRID_W, META_PAD)).reshape(N_HEADS_A // 2, 2 * GRID_W, META_PAD)
    return tab, bm


ATTN_B_UNROLL = 8


def _attn_b_kernel(sink_ref, q_ref, k_ref, v_ref, km_ref, vm_ref, bmeta_ref, o_ref, *, seq_len):
    kv = pl.program_id(1)
    nb = seq_len // WBLOCK
    span = 3 * WBLOCK
    group = N_HEADS_B // N_KV_B
    stack = group * WBLOCK
    lane = lax.broadcasted_iota(I32, (WBLOCK, LANES), 1)
    first = lane < HEAD_DIM
    km = km_ref[...]
    vm = vm_ref[...]
    bmeta = bmeta_ref[...]
    nt = (((1,), (1,)), ((), ()))
    qi = lax.broadcasted_iota(I32, (stack, span), 0) % WBLOCK
    kj = lax.broadcasted_iota(I32, (stack, span), 1)
    rel = kj - qi
    head_of_row = lax.broadcasted_iota(I32, (stack, 1), 0) // WBLOCK
    sink = jnp.zeros((stack, 1), F32)
    for g in range(group):
        sink = jnp.where(head_of_row == g, sink_ref[kv * group + g], sink)

    def blk_body(n, carry):
        start = jnp.clip((n - 1) * WBLOCK, 0, seq_len - span)
        start = pl.multiple_of(start, WBLOCK)
        q0 = pl.multiple_of(n * WBLOCK, WBLOCK)
        kb = k_ref[pl.ds(start, span), :]
        vb = v_ref[pl.ds(start, span), :]
        parts = []
        for c in range(group // 2):
            q = q_ref[pl.ds(q0, WBLOCK), c * LANES:(c + 1) * LANES]
            zero = jnp.zeros_like(q)
            parts += [jnp.where(first, q, zero), jnp.where(first, zero, q)]
        qs = jnp.concatenate(parts, axis=0)
        s = lax.dot_general(qs, kb, nt, preferred_element_type=F32)
        s = jnp.where(jnp.abs(rel + (start - q0)) <= WINDOW, s, NEG_INF)
        sm = lax.dot_general(qs, km, nt, preferred_element_type=F32) + bmeta
        m = jnp.maximum(jnp.max(s, axis=-1, keepdims=True), jnp.max(sm, axis=-1, keepdims=True))
        m = jnp.maximum(m, sink)
        p = jnp.exp(s - m)
        pm = jnp.exp(sm - m)
        l = jnp.sum(p, axis=-1, keepdims=True) + jnp.sum(pm, axis=-1, keepdims=True) + jnp.exp(sink - m)
        o = (jnp.dot(p.astype(BF16), vb, preferred_element_type=F32)
             + jnp.dot(pm.astype(BF16), vm, preferred_element_type=F32)) / l
        for c in range(group // 2):
            base = 2 * c * WBLOCK
            o_ref[pl.ds(q0, WBLOCK), c * LANES:(c + 1) * LANES] = jnp.where(
                first, o[base:base + WBLOCK], o[base + WBLOCK:base + 2 * WBLOCK]).astype(o_ref.dtype)
        return carry

    lax.fori_loop(0, nb, blk_body, 0, unroll=ATTN_B_UNROLL)


def _attn_b(proj, meta_proj, sinks, batch, seq_len):
    assert seq_len % (WBLOCK * ATTN_B_UNROLL) == 0 and seq_len >= 3 * WBLOCK
    n = batch * seq_len
    qw = WIDTH_B // N_KV_B
    bmeta = jnp.where(jnp.arange(META_PAD) < N_META, 0.0, NEG_INF).astype(F32)[None, :]
    kernel = functools.partial(_attn_b_kernel, seq_len=seq_len)
    grid_spec = pltpu.PrefetchScalarGridSpec(
        num_scalar_prefetch=1,
        grid=(batch, N_KV_B),
        in_specs=[
            pl.BlockSpec((seq_len, qw), lambda b, k, s: (b, SEG_QB // qw + k)),
            pl.BlockSpec((seq_len, LANES), lambda b, k, s: (b, SEG_KB // LANES + k)),
            pl.BlockSpec((seq_len, LANES), lambda b, k, s: (b, SEG_VB // LANES + k)),
            pl.BlockSpec((META_PAD, LANES), lambda b, k, s: (0, SEG_KB // LANES + k)),
            pl.BlockSpec((META_PAD, LANES), lambda b, k, s: (0, SEG_VB // LANES + k)),
            pl.BlockSpec((1, META_PAD), lambda b, k, s: (0, 0)),
        ],
        out_specs=pl.BlockSpec((seq_len, qw), lambda b, k, s: (b, k)),
    )
    return pl.pallas_call(
        kernel,
        grid_spec=grid_spec,
        out_shape=jax.ShapeDtypeStruct((n, WIDTH_B), BF16),
        compiler_params=_cparams(("parallel", "parallel")),
        name="attn_b",
    )(sinks.astype(F32), proj, proj, proj, meta_proj, meta_proj, bmeta)


def _pack_halves(a):
    w = a.shape[1] // 2
    lo = pltpu.bitcast(a[:, :w].astype(BF16).astype(F32), U32)
    hi = pltpu.bitcast(a[:, w:].astype(BF16).astype(F32), U32)
    return (lo >> 16) | (hi & jnp.uint32(0xFFFF0000))


def _unpack_halves(u):
    lo = pltpu.bitcast(u << 16, F32)
    hi = pltpu.bitcast(u & jnp.uint32(0xFFFF0000), F32)
    return lo, hi


def _post_kernel(oa_ref, ob_ref, x_ref, ga_ref, gb_ref, wo_ref, gm_ref, wrh_ref, wrl_ref, br_ref,
                 tri_ref, cnt0_ref, x1_ref, xn_ref, ei_ref, gate_ref, cnt_ref, run_ref):
    i = pl.program_id(0)

    @pl.when(i == 0)
    def _():
        run_ref[...] = cnt0_ref[...]

    def normed(ref, g_ref):
        a = ref[...].astype(F32)
        ms = jnp.mean(a * a, axis=-1, keepdims=True)
        return (a * lax.rsqrt(ms + NORM_EPS) * g_ref[...]).astype(BF16)

    wa = oa_ref.shape[1]
    mix = (jnp.dot(normed(oa_ref, ga_ref), wo_ref[:wa, :], preferred_element_type=F32)
           + jnp.dot(normed(ob_ref, gb_ref), wo_ref[wa:, :], preferred_element_type=F32))
    x1 = x_ref[...] + mix
    x1_ref[...] = x1
    ms = jnp.mean(x1 * x1, axis=-1, keepdims=True)
    xn = x1 * lax.rsqrt(ms + NORM_EPS) * gm_ref[...]
    xn_ref[...] = _pack_halves(xn)

    xh = xn.astype(BF16)
    xl = (xn - xh.astype(F32)).astype(BF16)
    logits = (jnp.dot(xh, wrh_ref[...], preferred_element_type=F32)
              + jnp.dot(xl, wrh_ref[...], preferred_element_type=F32)
              + jnp.dot(xh, wrl_ref[...], preferred_element_type=F32)) + br_ref[...]

    tm = logits.shape[0]
    lane = lax.broadcasted_iota(I32, (tm, LANES), 1)
    lanef = lane.astype(F32)
    work = logits
    vals, idxs = [], []
    chosen = jnp.zeros((tm, LANES), F32)
    for _ in range(TOP_K):
        mk = jnp.max(work, axis=-1, keepdims=True)
        ik = jnp.min(jnp.where(work == mk, lanef, float(LANES)), axis=-1, keepdims=True).astype(I32)
        hit = lane == ik
        work = jnp.where(hit, -jnp.inf, work)
        chosen = jnp.where(hit, 1.0, chosen)
        vals.append(mk)
        idxs.append(ik)
    ex = [jnp.exp(v - vals[0]) for v in vals]
    den = ex[0] + ex[1] + ex[2] + ex[3]

    prefix = jnp.dot(tri_ref[...], chosen.astype(BF16), preferred_element_type=F32) + run_ref[...]
    run_ref[...] = run_ref[...] + jnp.sum(chosen, axis=0, keepdims=True)
    cnt_ref[...] = run_ref[...]

    ei = jnp.zeros((tm, LANES), I32)
    gates = jnp.zeros((tm, LANES), F32)
    for k in range(TOP_K):
        rank = jnp.sum(jnp.where(lane == idxs[k], prefix, 0.0), axis=-1, keepdims=True).astype(I32)
        ei = jnp.where(lane == k, idxs[k], ei)
        ei = jnp.where(lane == TOP_K + k, rank, ei)
        gates = jnp.where(lane == k, ex[k] / den, gates)
    ei_ref[...] = ei
    gate_ref[...] = gates


def _post(oa, ob, x2d, ga, gb, wo, gm, wrh, wrl, br, cnt0):
    n, d = x2d.shape
    tm = 256
    assert n % tm == 0
    tri = jnp.asarray(np.tril(np.ones((tm, tm), np.float32), -1), BF16)
    row = lambda i: (i, 0)
    fixed = lambda i: (0, 0)
    return pl.pallas_call(
        _post_kernel,
        grid=(n // tm,),
        in_specs=[
            pl.BlockSpec((tm, oa.shape[1]), row),
            pl.BlockSpec((tm, ob.shape[1]), row),
            pl.BlockSpec((tm, d), row),
            pl.BlockSpec((1, oa.shape[1]), fixed),
            pl.BlockSpec((1, ob.shape[1]), fixed),
            pl.BlockSpec(wo.shape, fixed),
            pl.BlockSpec((1, d), fixed),
            pl.BlockSpec((d, LANES), fixed),
            pl.BlockSpec((d, LANES), fixed),
            pl.BlockSpec((1, LANES), fixed),
            pl.BlockSpec((tm, tm), fixed),
            pl.BlockSpec((1, LANES), fixed),
        ],
        out_specs=[
            pl.BlockSpec((tm, d), row),
            pl.BlockSpec((tm, d // 2), row),
            pl.BlockSpec((tm, LANES), row),
            pl.BlockSpec((tm, LANES), row),
            pl.BlockSpec((1, LANES), fixed),
        ],
        out_shape=[
            jax.ShapeDtypeStruct((n, d), F32),
            jax.ShapeDtypeStruct((n, d // 2), U32),
            jax.ShapeDtypeStruct((n, LANES), I32),
            jax.ShapeDtypeStruct((n, LANES), F32),
            jax.ShapeDtypeStruct((1, LANES), F32),
        ],
        scratch_shapes=[pltpu.VMEM((1, LANES), F32)],
        compiler_params=_cparams(("arbitrary",)),
        name="post",
    )(oa, ob, x2d, ga, gb, wo, gm, wrh, wrl, br, tri, cnt0)


ROW_DMA_UNROLL = 4


def _dispatch_kernel(slot_ref, xn_ref, xs_in_ref, xs_ref, sems, *, tk):
    del xs_in_ref

    def row_copy(t, k):
        dst = slot_ref[0, 0, t * TOP_K + k]
        return pltpu.make_async_copy(xn_ref.at[pl.ds(t, 1)], xs_ref.at[pl.ds(dst, 1)], sems.at[k])

    def issue(t, carry):
        for k in range(TOP_K):
            row_copy(t, k).start(priority=k % 2)
        return carry

    lax.fori_loop(0, tk, issue, 0, unroll=ROW_DMA_UNROLL)
    for k in range(TOP_K):
        pltpu.make_async_copy(xn_ref, xs_ref.at[pl.ds(0, tk)], sems.at[k]).wait()


def _dispatch(slots, xn, xs):
    n, w = xn.shape
    tk = 512
    assert n % tk == 0
    slots3 = slots.reshape(n // tk, 1, tk * TOP_K)
    kernel = functools.partial(_dispatch_kernel, tk=tk)
    return pl.pallas_call(
        kernel,
        grid=(n // tk,),
        in_specs=[
            pl.BlockSpec((1, 1, tk * TOP_K), lambda i: (i, 0, 0), memory_space=pltpu.SMEM),
            pl.BlockSpec((tk, w), lambda i: (i, 0)),
            pl.BlockSpec(memory_space=pl.ANY),
        ],
        out_specs=pl.BlockSpec(memory_space=pl.ANY),
        out_shape=jax.ShapeDtypeStruct(xs.shape, xs.dtype),
        input_output_aliases={2: 0},
        scratch_shapes=[pltpu.SemaphoreType.DMA((TOP_K,))],
        compiler_params=_cparams(("arbitrary",)),
        name="dispatch",
    )(slots3, xn, xs)


MOE_TM = 512
MOE_TF = 1024
FP8 = jnp.float8_e4m3fn
FP8_TOP = 224.0
ACT_SCALE = 4.0


def _pow2_scale(amax):
    safe = jnp.where(amax > 0, amax, FP8_TOP)
    return jnp.exp2(jnp.floor(jnp.log2(FP8_TOP / safe)))


def _quantize_kernel(w_ref, q_ref, s_ref):
    w = w_ref[0]
    scale = _pow2_scale(jnp.max(jnp.abs(w), axis=(0, 1), keepdims=True))
    q_ref[0] = (w * scale).astype(FP8)
    s_ref[...] = jnp.broadcast_to(1.0 / scale, s_ref.shape)


def _fp8_expert_weights(w, block):
    n_e, a, b = w.shape
    ta, tb = block
    assert a % ta == 0 and b % tb == 0 and (a == ta or b == tb)
    tiles = (a // ta) * (b // tb)
    tile_index = (lambda e, t: (e, t, 0)) if b == tb else (lambda e, t: (e, 0, t))
    q, s = pl.pallas_call(
        _quantize_kernel,
        grid=(n_e, tiles),
        in_specs=[pl.BlockSpec((1, ta, tb), tile_index)],
        out_specs=[pl.BlockSpec((1, ta, tb), tile_index),
                   pl.BlockSpec((1, 1, 8, LANES), lambda e, t: (e, t, 0, 0))],
        out_shape=[jax.ShapeDtypeStruct(w.shape, FP8), jax.ShapeDtypeStruct((n_e, tiles, 8, LANES), F32)],
        compiler_params=_cparams(("parallel", "parallel")),
        name="quantize",
    )(w)
    return q, s[:, :, 0, 0].reshape(-1)


def _experts_kernel(bexp_ref, bval_ref, sg_ref, su_ref, sd_ref, xs_ref, wg_ref, bg_ref, wu_ref, bu_ref,
                    wd_ref, bd_ref, ys_ref, lo_ref, hi_ref, rinv_ref, acc_ref):
    i = pl.program_id(0)
    j = pl.program_id(1)
    nvalid = bval_ref[i]
    tile = bexp_ref[i] * pl.num_programs(1) + j

    @pl.when(jnp.logical_and(nvalid == 0, j == 0))
    def _():
        ys_ref[...] = jnp.zeros_like(ys_ref)

    @pl.when(nvalid > 0)
    def _():
        @pl.when(j == 0)
        def _():
            u = xs_ref[...]
            rows = lax.broadcasted_iota(I32, u.shape, 0)
            u = jnp.where(rows < nvalid, u, jnp.zeros_like(u))
            lo, hi = _unpack_halves(u)
            amax = jnp.maximum(jnp.max(jnp.abs(lo), axis=-1, keepdims=True),
                               jnp.max(jnp.abs(hi), axis=-1, keepdims=True))
            rs = _pow2_scale(amax)
            lo_ref[...] = (lo * rs).astype(FP8)
            hi_ref[...] = (hi * rs).astype(FP8)
            rinv_ref[...] = jnp.broadcast_to(1.0 / rs, rinv_ref.shape)
            acc_ref[...] = jnp.zeros_like(acc_ref)

        half = lo_ref.shape[1]
        lo = lo_ref[...]
        hi = hi_ref[...]
        rinv = rinv_ref[:, :1]
        hg = (jnp.dot(lo, wg_ref[0, :half, :], preferred_element_type=F32)
              + jnp.dot(hi, wg_ref[0, half:, :], preferred_element_type=F32)) * (rinv * sg_ref[tile]) + bg_ref[0]
        hu = (jnp.dot(lo, wu_ref[0, :half, :], preferred_element_type=F32)
              + jnp.dot(hi, wu_ref[0, half:, :], preferred_element_type=F32)) * (rinv * su_ref[tile]) + bu_ref[0]
        g = jnp.minimum(hg, SWIGLU_LIMIT)
        u = jnp.clip(hu, -SWIGLU_LIMIT, SWIGLU_LIMIT)
        act = (u + 1.0) * (g * jax.nn.sigmoid(SWIGLU_ALPHA * g))
        down = jnp.dot((act * ACT_SCALE).astype(FP8), wd_ref[0], preferred_element_type=F32)
        acc_ref[...] += down * (sd_ref[tile] * (1.0 / ACT_SCALE))

        @pl.when(j == pl.num_programs(1) - 1)
        def _():
            ys_ref[...] = _pack_halves(acc_ref[...] + bd_ref[0])


def _experts(blk_exp, blk_valid, xs, wg, sg, bg, wu, su, bu, wd, sd, bd, tm, tf):
    p, half = xs.shape
    d = 2 * half
    dff = wg.shape[2]
    assert p % tm == 0 and dff % tf == 0
    nblk = p // tm
    nf = dff // tf

    def jeff(i, j, bval):
        return jnp.where(bval[i] > 0, j, nf - 1)

    def wspec(shape, imap):
        return pl.BlockSpec(shape, lambda i, j, be, bv, *_: imap(be[i], jeff(i, j, bv)))

    grid_spec = pltpu.PrefetchScalarGridSpec(
        num_scalar_prefetch=5,
        grid=(nblk, nf),
        in_specs=[
            pl.BlockSpec((tm, half), lambda i, j, *_: (i, 0)),
            wspec((1, d, tf), lambda e, f: (e, 0, f)),
            wspec((1, 1, tf), lambda e, f: (e, 0, f)),
            wspec((1, d, tf), lambda e, f: (e, 0, f)),
            wspec((1, 1, tf), lambda e, f: (e, 0, f)),
            wspec((1, tf, d), lambda e, f: (e, f, 0)),
            wspec((1, 1, d), lambda e, f: (e, 0, 0)),
        ],
        out_specs=pl.BlockSpec((tm, half), lambda i, j, *_: (i, 0)),
        scratch_shapes=[pltpu.VMEM((tm, half), FP8), pltpu.VMEM((tm, half), FP8),
                        pltpu.VMEM((tm, LANES), F32), pltpu.VMEM((tm, d), F32)],
    )
    return pl.pallas_call(
        _experts_kernel,
        grid_spec=grid_spec,
        out_shape=jax.ShapeDtypeStruct((p, half), U32),
        compiler_params=_cparams(("arbitrary", "arbitrary")),
        name="experts",
    )(blk_exp, blk_valid, sg, su, sd, xs, wg, bg, wu, bu, wd, bd)


def _combine_kernel(slot_ref, slot_next_ref, x1_ref, gate_ref, ys_ref, o_ref, buf_ref, sems, *, tk):
    i = pl.program_id(0)
    cur = i % 2

    def gather(slots, b):
        def issue(t, carry):
            for k in range(TOP_K):
                src = slots[0, 0, t * TOP_K + k]
                pltpu.make_async_copy(ys_ref.at[pl.ds(src, 1)], buf_ref.at[b, k, pl.ds(t, 1)],
                                      sems.at[b, k]).start(priority=k % 2)
            return carry
        lax.fori_loop(0, tk, issue, 0, unroll=ROW_DMA_UNROLL)

    @pl.when(i == 0)
    def _():
        gather(slot_ref, 0)

    @pl.when(i + 1 < pl.num_programs(0))
    def _():
        gather(slot_next_ref, 1 - cur)

    half = buf_ref.shape[3]
    gates = gate_ref[...]
    acc_lo = x1_ref[:, :half]
    acc_hi = x1_ref[:, half:]
    for k in range(TOP_K):
        pltpu.make_async_copy(ys_ref.at[pl.ds(0, tk)], buf_ref.at[cur, k], sems.at[cur, k]).wait()
        lo, hi = _unpack_halves(buf_ref[cur, k])
        g = gates[:, k:k + 1]
        acc_lo = acc_lo + g * lo
        acc_hi = acc_hi + g * hi
    o_ref[:, :half] = acc_lo
    o_ref[:, half:] = acc_hi


def _combine(slots, x1, gates, ys):
    n, d = x1.shape
    tk = 256
    assert n % tk == 0
    steps = n // tk
    slots3 = slots.reshape(steps, 1, tk * TOP_K)
    kernel = functools.partial(_combine_kernel, tk=tk)
    return pl.pallas_call(
        kernel,
        grid=(steps,),
        in_specs=[
            pl.BlockSpec((1, 1, tk * TOP_K), lambda i: (i, 0, 0), memory_space=pltpu.SMEM),
            pl.BlockSpec((1, 1, tk * TOP_K), lambda i: (jnp.minimum(i + 1, steps - 1), 0, 0),
                         memory_space=pltpu.SMEM),
            pl.BlockSpec((tk, d), lambda i: (i, 0)),
            pl.BlockSpec((tk, LANES), lambda i: (i, 0)),
            pl.BlockSpec(memory_space=pl.ANY),
        ],
        out_specs=pl.BlockSpec((tk, d), lambda i: (i, 0)),
        out_shape=jax.ShapeDtypeStruct((n, d), F32),
        scratch_shapes=[pltpu.VMEM((2, TOP_K, tk, d // 2), U32), pltpu.SemaphoreType.DMA((2, TOP_K))],
        compiler_params=_cparams(("arbitrary",)),
        name="combine",
    )(slots3, slots3, x1, gates, ys)


def _dup_kv_heads(w):
    d = w.shape[0]
    w = w.reshape(d, N_KV_B, 1, HEAD_DIM)
    return jnp.broadcast_to(w, (d, N_KV_B, 2, HEAD_DIM)).reshape(d, 2 * KV_WIDTH_B)


def kernel(x_prompt, x_sample, meta_tokens, norm_attn, w_in, q_norm_a, k_norm_a, rpb, rpb_meta, q_norm_b, k_norm_b, sinks, out_norm_a, out_norm_b, w_out, norm_mlp, w_router, b_router, w_gate, b_gate, w_up, b_up, w_down, b_down):
    depth = norm_attn.shape[0]
    assert depth == 1
    d = x_prompt.shape[-1]
    groups = [(x_prompt.reshape(-1, d), x_prompt.shape[0], x_prompt.shape[1]),
              (x_sample.reshape(-1, d), x_sample.shape[0], x_sample.shape[1])]

    wi = w_in[0]
    w_cat = jnp.concatenate([wi[:, :SEG_KB], _dup_kv_heads(wi[:, SEG_KB:SEG_KB + KV_WIDTH_B]),
                             _dup_kv_heads(wi[:, SEG_KB + KV_WIDTH_B:])], axis=1).astype(BF16)
    scale = HEAD_DIM ** -0.5
    gcol = jnp.concatenate([
        jnp.tile(q_norm_a[0].astype(F32) * scale, N_HEADS_A), jnp.tile(k_norm_a[0].astype(F32), N_HEADS_A),
        jnp.ones((WIDTH_A,), F32),
        jnp.tile(q_norm_b[0].astype(F32) * scale, N_HEADS_B), jnp.tile(k_norm_b[0].astype(F32), 2 * N_KV_B),
        jnp.ones((2 * KV_WIDTH_B,), F32)])[None, :]
    bd = jnp.asarray(np.kron(np.eye(LANES // HEAD_DIM, dtype=np.float32),
                             np.ones((HEAD_DIM, HEAD_DIM), np.float32)), BF16)
    g_attn = norm_attn[0].astype(F32)[None, :]
    bias_tab, bias_meta = _bias_tables_a(rpb[0], rpb_meta[0])
    wo = w_out[0].astype(BF16)
    ga = out_norm_a[0].astype(F32)[None, :]
    gb = out_norm_b[0].astype(F32)[None, :]
    gm = norm_mlp[0].astype(F32)[None, :]
    wr = jnp.zeros((d, LANES), F32).at[:, :N_EXPERTS].set(w_router[0].astype(F32))
    wrh = wr.astype(BF16)
    wrl = (wr - wrh.astype(F32)).astype(BF16)
    br = jnp.full((1, LANES), NEG_INF, F32).at[0, :N_EXPERTS].set(b_router[0].astype(F32))
    wg, sg = _fp8_expert_weights(w_gate[0].astype(F32), (d, MOE_TF))
    wu, su = _fp8_expert_weights(w_up[0].astype(F32), (d, MOE_TF))
    wd, sd = _fp8_expert_weights(w_down[0].astype(F32), (MOE_TF, d))
    bg = b_gate[0].astype(F32)[:, None, :]
    bu = b_up[0].astype(F32)[:, None, :]
    bdn = b_down[0].astype(F32)[:, None, :]

    meta_x = jnp.zeros((META_PAD, d), F32).at[:N_META].set(meta_tokens.astype(F32))
    meta_pos = np.minimum(np.arange(META_PAD), N_META - 1)
    meta_proj = _proj(meta_x, META_PAD, meta_pos, g_attn, w_cat, gcol, bd)
    meta_proj = jnp.where(jnp.arange(META_PAD)[:, None] < N_META, meta_proj, jnp.zeros_like(meta_proj))

    cnt = jnp.zeros((1, LANES), F32)
    staged = []
    for x2d, batch, seq_len in groups:
        pos = N_META + np.arange(seq_len)
        proj = _proj(x2d, seq_len, pos, g_attn, w_cat, gcol, bd)
        oa = _attn_a(proj, meta_proj, bias_tab, bias_meta, batch, seq_len)
        ob = _attn_b(proj, meta_proj, sinks[0], batch, seq_len)
        x1, xn, ei, gates, cnt = _post(oa, ob, x2d, ga, gb, wo, gm, wrh, wrl, br, cnt)
        staged.append((x1, xn, ei, gates))

    total = sum(g[0].shape[0] for g in groups) * TOP_K
    nblk = (total + N_EXPERTS * (MOE_TM - 1) + MOE_TM - 1) // MOE_TM
    counts = cnt[0, :N_EXPERTS].astype(I32)
    padded = (counts + MOE_TM - 1) // MOE_TM * MOE_TM
    pends = jnp.cumsum(padded)
    pstarts = pends - padded
    blk_start = jnp.arange(nblk, dtype=I32) * MOE_TM
    blk_exp = jnp.minimum(jnp.sum((blk_start[:, None] >= pends[None, :]).astype(I32), axis=1), N_EXPERTS - 1)
    blk_valid = jnp.clip(counts[blk_exp] - (blk_start - pstarts[blk_exp]), 0, MOE_TM)
    blk_valid = jnp.where(blk_start < pends[-1], blk_valid, 0).astype(I32)
    last_used = jnp.maximum(pends[-1] // MOE_TM - 1, 0)
    blk_exp = jnp.where(blk_start < pends[-1], blk_exp, blk_exp[last_used]).astype(I32)

    slots = [pstarts[ei[:, :TOP_K]] + ei[:, TOP_K:2 * TOP_K] for (_, _, ei, _) in staged]

    xs = jnp.zeros((nblk * MOE_TM, d // 2), U32)
    for (x1, xn, ei, gates), sl in zip(staged, slots):
        xs = _dispatch(sl, xn, xs)
    ys = _experts(blk_exp, blk_valid, xs, wg, sg, bg, wu, su, bu, wd, sd, bdn, MOE_TM, MOE_TF)

    outs = []
    for (x1, xn, ei, gates), sl, (x2d, batch, seq_len) in zip(staged, slots, groups):
        y = _combine(sl, x1, gates, ys)
        outs.append(y.reshape(batch, seq_len, d))
    return tuple(outs)
```
